```python
import math
import numpy as np
import jax
import jax.numpy as jnp
from jax import lax


D_MODEL = 1024
BATCH = 16
SEQ = 4096
DEPTH = 4

GRID_W = 64
CTX_LEN = 256
F32 = jnp.float32
NORM_EPS = 1e-6
ROPE_THETA = 10000.0
Q_BLOCK = 128
N_MOD = 6

SSD_HEADS = 16
SSD_HEAD_DIM = 64
SSD_INNER = SSD_HEADS * SSD_HEAD_DIM
SSD_GROUPS = 4
SSD_STATE = 128
SSD_BC = SSD_GROUPS * SSD_STATE
XBC_DIM = SSD_INNER + 2 * SSD_BC
SSD_CONV = 5
SSD_CHUNK = 128

NA_HEADS = 8
NA_HEAD_DIM = 64
NA_DIM = NA_HEADS * NA_HEAD_DIM
NA_ROWS = 8
NA_COLS = 16

EVEN_SIZES = (NA_DIM, NA_DIM, NA_DIM, SSD_INNER, XBC_DIM, SSD_HEADS, SSD_HEADS)
EVEN_IN = sum(EVEN_SIZES)
EVEN_MIX = NA_DIM + SSD_INNER

MLA_HEADS = 8
MLA_Q_LORA = 384
MLA_KV_LORA = 256
MLA_NOPE = 64
MLA_ROPE = 32
MLA_V = 64
MLA_SCALE = (MLA_NOPE + MLA_ROPE) ** -0.5

GQA_HEADS = 8
GQA_KV_HEADS = 2
GQA_GROUP = GQA_HEADS // GQA_KV_HEADS
GQA_HEAD_DIM = 64
GQA_SCALE = GQA_HEAD_DIM ** -0.5

ODD_Q_SIZES = (MLA_Q_LORA, GQA_HEADS * GQA_HEAD_DIM)
ODD_KV_SIZES = (MLA_KV_LORA, MLA_ROPE, GQA_KV_HEADS * GQA_HEAD_DIM, GQA_KV_HEADS * GQA_HEAD_DIM)
ODD_Q_COLS = sum(ODD_Q_SIZES)
ODD_IN = ODD_Q_COLS + sum(ODD_KV_SIZES)
ODD_MIX = MLA_HEADS * MLA_V + GQA_HEADS * GQA_HEAD_DIM

MOE_GROUPS = 4
MOE_EXPERTS_PER_GROUP = 8
MOE_EXPERTS = MOE_GROUPS * MOE_EXPERTS_PER_GROUP
MOE_TOPK = 2
MOE_HIDDEN = 512
MOE_BLOCK = 128

N_EVEN = (DEPTH + 1) // 2
N_ODD = DEPTH // 2
DEEPNORM_ALPHA = (2.0 * DEPTH) ** 0.25
DEEPNORM_BETA = (8.0 * DEPTH) ** -0.25

kernel_name = 'hybrid_ssd_natten_mla_gqa_hmoe_trunk'


def _split(t, sizes):
    return jnp.split(t, [int(s) for s in np.cumsum(sizes)[:-1]], axis=-1)


def layer_norm(x, g, b):
    xf = x.astype(F32)
    mu = jnp.mean(xf, -1, keepdims=True)
    var = jnp.mean(jnp.square(xf - mu), -1, keepdims=True)
    return ((xf - mu) * lax.rsqrt(var + NORM_EPS) * g + b).astype(x.dtype)


def rms_norm(x, g):
    xf = x.astype(F32)
    return (xf * lax.rsqrt(jnp.mean(jnp.square(xf), -1, keepdims=True) + NORM_EPS) * g).astype(x.dtype)


def modulate(x, shift, scale):
    return x * (1.0 + scale) + shift


def axial_rope(u):
    n, dim = u.shape[1], u.shape[-1]
    half = dim // 2
    t = jnp.arange(n, dtype=jnp.int32)
    inv_freq = jnp.power(ROPE_THETA, -jnp.arange(0, half, 2, dtype=F32) / half)

    def rotate(v, pos):
        ang = pos.astype(F32)[:, None] * inv_freq
        cos, sin = jnp.cos(ang)[:, None, :], jnp.sin(ang)[:, None, :]
        v1, v2 = jnp.split(v.astype(F32), 2, axis=-1)
        return jnp.concatenate([v1 * cos - v2 * sin, v1 * sin + v2 * cos], -1)

    out = jnp.concatenate([rotate(u[..., :half], t // GRID_W), rotate(u[..., half:], t % GRID_W)], -1)
    return out.astype(u.dtype)


def block_attention(q, k, v, scale):
    b, lq = q.shape[:2]
    nb = lq // Q_BLOCK
    qb = q.reshape((b, nb, Q_BLOCK) + q.shape[2:]).swapaxes(0, 1)

    def one(qi):
        s = jnp.einsum('bqhrd,bkhd->bhrqk', qi, k).astype(F32) * scale
        p = jax.nn.softmax(s, axis=-1).astype(v.dtype)
        return jnp.einsum('bhrqk,bkhe->bqhre', p, v)

    out = lax.map(one, qb)
    return out.swapaxes(0, 1).reshape((b, lq) + out.shape[3:])


def neighbourhood_attention(q, k, v, k_ctx, v_ctx, rpb, rows):
    b, n, heads, dh = q.shape
    kh, kw = min(NA_ROWS, rows), NA_COLS
    scale = dh ** -0.5
    col_start = np.clip(np.arange(GRID_W) - kw // 2, 0, GRID_W - kw)
    col_idx = col_start[:, None] + np.arange(kw)[None, :]
    dc = col_idx - np.arange(GRID_W)[:, None] + (NA_COLS - 1)
    bias_cols = rpb[:, :, dc]
    qg = q.reshape(b, rows, GRID_W, heads, dh).swapaxes(0, 1)
    kg = k.reshape(b, rows, GRID_W, heads, dh)
    vg = v.reshape(b, rows, GRID_W, heads, dh)

    def row_block(args):
        q_row, r = args
        r0 = jnp.clip(r - kh // 2, 0, rows - kh)
        k_win = lax.dynamic_slice_in_dim(kg, r0, kh, axis=1)[:, :, col_idx]
        v_win = lax.dynamic_slice_in_dim(vg, r0, kh, axis=1)[:, :, col_idx]
        dr = r0 + jnp.arange(kh) - r + (NA_ROWS - 1)
        bias = jnp.take(bias_cols, dr, axis=1).transpose(0, 2, 1, 3).astype(F32)
        s_loc = jnp.einsum('bjhd,bkjwhd->bhjkw', q_row, k_win).astype(F32) * scale + bias
        s_ctx = jnp.einsum('bjhd,bchd->bhjc', q_row, k_ctx).astype(F32) * scale
        s = jnp.concatenate([s_loc.reshape(b, heads, GRID_W, kh * kw), s_ctx], -1)
        p = jax.nn.softmax(s, axis=-1).astype(v.dtype)
        p_loc = p[..., :kh * kw].reshape(b, heads, GRID_W, kh, kw)
        return (jnp.einsum('bhjkw,bkjwhd->bjhd', p_loc, v_win)
                + jnp.einsum('bhjc,bchd->bjhd', p[..., kh * kw:], v_ctx))

    out = lax.map(row_block, (qg, jnp.arange(rows, dtype=jnp.int32)))
    return out.swapaxes(0, 1).reshape(b, n, heads * dh)


def depthwise_conv_centred(u, w, bias):
    taps, ch = w.shape
    out = lax.conv_general_dilated(u, w[:, None, :].astype(u.dtype), window_strides=(1,),
                                   padding=[(taps // 2, taps // 2)],
                                   dimension_numbers=('NWC', 'WIO', 'NWC'), feature_group_count=ch)
    return out + bias


def ssd_chunk_scan(xdt, a, bm, cm, h0):
    b, seq, heads, hp = xdt.shape
    groups, ns = bm.shape[2], bm.shape[3]
    rep = heads // groups
    q = SSD_CHUNK
    nc = seq // q
    x_c = xdt.reshape(b, nc, q, groups, rep, hp)
    a_cs = jnp.cumsum(a.astype(F32).reshape(b, nc, q, groups, rep), axis=2)
    b_c = bm.reshape(b, nc, q, groups, ns)
    c_c = cm.reshape(b, nc, q, groups, ns)
    seg = a_cs[:, :, :, None] - a_cs[:, :, None, :]
    lower = np.tril(np.ones((q, q), dtype=bool))[:, :, None, None]
    decay = jnp.exp(jnp.where(lower, seg, -jnp.inf))
    cb = jnp.einsum('bclgn,bcsgn->bclsg', c_c, b_c)
    y_diag = jnp.einsum('bclsgr,bcsgrp->bclgrp', decay * cb[..., None], x_c)
    decay_to_end = jnp.exp(a_cs[:, :, -1:] - a_cs)
    s_chunk = jnp.einsum('bclgn,bclgr,bclgrp->bcgrpn', b_c, decay_to_end, x_c)
    chunk_decay = jnp.exp(a_cs[:, :, -1])

    def step(h, inp):
        dec, s = inp
        return dec[..., None, None] * h + s, h

    h_last, h_in = lax.scan(step, h0, (jnp.moveaxis(chunk_decay, 1, 0), jnp.moveaxis(s_chunk, 1, 0)))
    h_in = jnp.moveaxis(h_in, 0, 1)
    y_off = jnp.einsum('bclgn,bcgrpn,bclgr->bclgrp', c_c, h_in, jnp.exp(a_cs))
    return (y_diag + y_off).reshape(b, seq, heads, hp), h_last


def ssd_direction(xs, bm, cm, dt, a_coef, h0, reverse):
    order = (lambda t: jnp.flip(t, axis=1)) if reverse else (lambda t: t)
    y, h_last = ssd_chunk_scan(order(xs * dt[..., None]), order(dt * a_coef), order(bm), order(cm), h0)
    return order(y).astype(xs.dtype), h_last


def even_mixer(h, hc, rows, w_in, w_out, conv_w, conv_b, dt_bias, a_log, d_skip, gnorm_w, rpb):
    b = h.shape[0]
    q, k, v, z, xbc, dtf, dtb = _split(h @ w_in, EVEN_SIZES)
    qc, kc, vc, zc, xbcc, dtfc, dtbc = _split(hc @ w_in, EVEN_SIZES)
    heads = lambda t: t.reshape(t.shape[0], t.shape[1], NA_HEADS, NA_HEAD_DIM)
    q, k, v, qc, kc, vc = heads(q), heads(k), heads(v), heads(qc), heads(kc), heads(vc)
    na_lat = neighbourhood_attention(q, k, v, kc, vc, rpb, rows)
    na_ctx = block_attention(qc[:, :, :, None], kc, vc, NA_HEAD_DIM ** -0.5).reshape(b, hc.shape[1], NA_DIM)

    def ssd_branch(xbc_, z_, dts, h0s):
        u = jax.nn.silu(depthwise_conv_centred(xbc_, conv_w, conv_b))
        xs, bm, cm = _split(u, (SSD_INNER, SSD_BC, SSD_BC))
        nb, n = xs.shape[:2]
        xs = xs.reshape(nb, n, SSD_HEADS, SSD_HEAD_DIM)
        bm = bm.reshape(nb, n, SSD_GROUPS, SSD_STATE)
        cm = cm.reshape(nb, n, SSD_GROUPS, SSD_STATE)
        y = d_skip[:, None] * xs
        finals = []
        for d in range(2):
            dt = jax.nn.softplus(dts[d].astype(F32) + dt_bias[d])
            yd, hd = ssd_direction(xs, bm, cm, dt, -jnp.exp(a_log[d].astype(F32)), h0s[d], d == 1)
            y = y + yd
            finals.append(hd)
        y = y.reshape(nb, n, SSD_INNER) * jax.nn.silu(z_)
        return rms_norm(y, gnorm_w), finals

    h0 = jnp.zeros((b, SSD_GROUPS, SSD_HEADS // SSD_GROUPS, SSD_HEAD_DIM, SSD_STATE), F32)
    ssd_ctx, ctx_states = ssd_branch(xbcc, zc, (dtfc, dtbc), (h0, h0))
    ssd_lat, _ = ssd_branch(xbc, z, (dtf, dtb), ctx_states)
    y_lat = jnp.concatenate([na_lat, ssd_lat], -1) @ w_out
    y_ctx = jnp.concatenate([na_ctx, ssd_ctx], -1) @ w_out
    return y_lat, y_ctx


def mla_queries(cq, norm_w, w_uq, rotary):
    b, n, _ = cq.shape
    q = (rms_norm(cq, norm_w) @ w_uq).reshape(b, n, MLA_HEADS, MLA_NOPE + MLA_ROPE)
    q_nope, q_rot = q[..., :MLA_NOPE], q[..., MLA_NOPE:]
    if rotary:
        q_rot = axial_rope(q_rot)
    return jnp.concatenate([q_nope, q_rot], -1)


def mla_keys_values(ckv, k_rot, norm_w, w_ukv, rotary):
    b, n, _ = ckv.shape
    kv = (rms_norm(ckv, norm_w) @ w_ukv).reshape(b, n, MLA_HEADS, MLA_NOPE + MLA_V)
    k_nope, v = kv[..., :MLA_NOPE], kv[..., MLA_NOPE:]
    k_rot = k_rot[:, :, None, :]
    if rotary:
        k_rot = axial_rope(k_rot)
    k = jnp.concatenate([k_nope, jnp.broadcast_to(k_rot, (b, n, MLA_HEADS, MLA_ROPE))], -1)
    return k, v


def odd_mixer(h, hc, w_in, w_out, mla_q_norm, mla_kv_norm, w_uq, w_ukv, gqa_q_norm, gqa_k_norm, ctx_queries):
    cq, gq, ckv, krot, gk, gv = _split(h @ w_in, ODD_Q_SIZES + ODD_KV_SIZES)
    if ctx_queries:
        cq_c, gq_c, ckv_c, krot_c, gk_c, gv_c = _split(hc @ w_in, ODD_Q_SIZES + ODD_KV_SIZES)
    else:
        ckv_c, krot_c, gk_c, gv_c = _split(hc @ w_in[:, ODD_Q_COLS:], ODD_KV_SIZES)
    q_heads = lambda t: t.reshape(t.shape[0], t.shape[1], GQA_HEADS, GQA_HEAD_DIM)
    kv_heads = lambda t: t.reshape(t.shape[0], t.shape[1], GQA_KV_HEADS, GQA_HEAD_DIM)
    q_groups = lambda t: t.reshape(t.shape[0], t.shape[1], GQA_KV_HEADS, GQA_GROUP, GQA_HEAD_DIM)
    mk_c, mv_c = mla_keys_values(ckv_c, krot_c, mla_kv_norm, w_ukv, rotary=False)
    gk_c = rms_norm(kv_heads(gk_c), gqa_k_norm)
    gv_c = kv_heads(gv_c)
    mk, mv = mla_keys_values(ckv, krot, mla_kv_norm, w_ukv, rotary=True)
    mq = mla_queries(cq, mla_q_norm, w_uq, rotary=True)
    gq = q_groups(axial_rope(rms_norm(q_heads(gq), gqa_q_norm)))
    gk = axial_rope(rms_norm(kv_heads(gk), gqa_k_norm))
    gv = kv_heads(gv)
    cat = lambda first, second: jnp.concatenate([first, second], axis=1)

    def mix(mq_, gq_, mk_, mv_, gk_, gv_):
        bsz, m = mq_.shape[:2]
        o_mla = block_attention(mq_[:, :, :, None], mk_, mv_, MLA_SCALE).reshape(bsz, m, MLA_HEADS * MLA_V)
        o_gqa = block_attention(gq_, gk_, gv_, GQA_SCALE).reshape(bsz, m, GQA_HEADS * GQA_HEAD_DIM)
        return jnp.concatenate([o_mla, o_gqa], -1) @ w_out

    y_lat = mix(mq, gq, cat(mk_c, mk), cat(mv_c, mv), cat(gk_c, gk), cat(gv_c, gv))
    if not ctx_queries:
        return y_lat, None
    mq_c = mla_queries(cq_c, mla_q_norm, w_uq, rotary=False)
    gq_c = q_groups(rms_norm(q_heads(gq_c), gqa_q_norm))
    y_ctx = mix(mq_c, gq_c, mk_c, mv_c, gk_c, gv_c)
    return y_lat, y_ctx


def grouped_expert_mlp(h, expert, gate, w1, w3, w2):
    t_tok, dm = h.shape
    n_exp = w1.shape[0]
    n_assign = t_tok * MOE_TOPK
    flat_e = expert.reshape(n_assign).astype(jnp.int32)
    flat_tok = jnp.repeat(jnp.arange(t_tok, dtype=jnp.int32), MOE_TOPK)
    flat_w = gate.reshape(n_assign)
    order = jnp.argsort(flat_e)
    se = flat_e[order]
    counts = jnp.bincount(flat_e, length=n_exp)
    starts = jnp.cumsum(counts) - counts
    pcounts = (counts + MOE_BLOCK - 1) // MOE_BLOCK * MOE_BLOCK
    pends = jnp.cumsum(pcounts)
    pstarts = pends - pcounts
    dest = pstarts[se] + (jnp.arange(n_assign, dtype=jnp.int32) - starts[se])
    nb = (n_assign + n_exp * (MOE_BLOCK - 1)) // MOE_BLOCK
    slot_tok = jnp.full((nb * MOE_BLOCK,), t_tok, jnp.int32).at[dest].set(flat_tok[order])
    slot_w = jnp.zeros((nb * MOE_BLOCK,), h.dtype).at[dest].set(flat_w[order].astype(h.dtype))
    blk_e = jnp.minimum(jnp.searchsorted(pends, jnp.arange(nb, dtype=jnp.int32) * MOE_BLOCK, side='right'),
                        n_exp - 1)
    hp = jnp.concatenate([h, jnp.zeros((1, dm), h.dtype)], 0)
    xb = hp[slot_tok].reshape(nb, MOE_BLOCK, dm)

    def expert_block(args):
        xi, e = args
        return (jax.nn.silu(xi @ w1[e]) * (xi @ w3[e])) @ w2[e]

    yb = lax.map(expert_block, (xb, blk_e)).reshape(nb * MOE_BLOCK, dm)
    out = jnp.zeros((t_tok + 1, dm), h.dtype).at[slot_tok].add(yb * slot_w[:, None])
    return out[:t_tok]


def hier_moe(h, wg, bg, we, be, w1, w3, w2):
    t_tok = h.shape[0]
    g_prob = jax.nn.softmax((h @ wg).astype(F32) + bg, axis=-1)
    g_gate, g_sel = lax.top_k(g_prob, 1)
    e_logits = ((h @ we).astype(F32) + be).reshape(t_tok, MOE_GROUPS, MOE_EXPERTS_PER_GROUP)
    e_in = jnp.take_along_axis(e_logits, g_sel[:, :, None], axis=1)[:, 0]
    e_top, e_idx = lax.top_k(e_in, MOE_TOPK)
    gate = g_gate * jax.nn.softmax(e_top, axis=-1)
    expert = g_sel * MOE_EXPERTS_PER_GROUP + e_idx
    return grouped_expert_mlp(h, expert, gate, w1, w3, w2)


def setup_inputs(seed: int = 0) -> dict:
    key = jax.random.key(seed)
    ks = iter(jax.random.split(key, 40))
    nrm = lambda shape, scale: jax.random.normal(next(ks), shape, F32) * scale
    dm = D_MODEL
    dt0 = jnp.exp(jax.random.uniform(next(ks), (N_EVEN, 2, SSD_HEADS), F32, math.log(1e-3), math.log(1e-1)))
    a0 = jax.random.uniform(next(ks), (N_EVEN, 2, SSD_HEADS), F32, 1.0, 16.0)
    return {
        'x': nrm((BATCH, SEQ, dm), 1.0),
        'c': nrm((BATCH, dm), 1.0),
        'ctx': nrm((BATCH, CTX_LEN, dm), 1.0),
        'c_ctx': nrm((dm,), 1.0),
        'ada_w': nrm((DEPTH, dm, N_MOD * dm), 0.5 * dm ** -0.5),
        'ada_b': nrm((DEPTH, N_MOD * dm), 0.02),
        'ln_g': 1.0 + nrm((DEPTH, 2, dm), 0.02),
        'ln_b': nrm((DEPTH, 2, dm), 0.02),
        'e_w_in': nrm((N_EVEN, dm, EVEN_IN), dm ** -0.5),
        'e_w_out': nrm((N_EVEN, EVEN_MIX, dm), DEEPNORM_BETA * EVEN_MIX ** -0.5),
        'e_conv_w': nrm((N_EVEN, SSD_CONV, XBC_DIM), SSD_CONV ** -0.5),
        'e_conv_b': nrm((N_EVEN, XBC_DIM), 0.01),
        'e_dt_bias': dt0 + jnp.log(-jnp.expm1(-dt0)),
        'e_a_log': jnp.log(a0),
        'e_d_skip': 1.0 + nrm((N_EVEN, SSD_HEADS), 0.02),
        'e_gnorm_w': 1.0 + nrm((N_EVEN, SSD_INNER), 0.02),
        'e_rpb': nrm((N_EVEN, NA_HEADS, 2 * NA_ROWS - 1, 2 * NA_COLS - 1), 0.02),
        'o_w_in': nrm((N_ODD, dm, ODD_IN), dm ** -0.5),
        'o_w_out': nrm((N_ODD, ODD_MIX, dm), DEEPNORM_BETA * ODD_MIX ** -0.5),
        'o_mla_q_norm': 1.0 + nrm((N_ODD, MLA_Q_LORA), 0.02),
        'o_mla_kv_norm': 1.0 + nrm((N_ODD, MLA_KV_LORA), 0.02),
        'o_w_uq': nrm((N_ODD, MLA_Q_LORA, MLA_HEADS * (MLA_NOPE + MLA_ROPE)), MLA_Q_LORA ** -0.5),
        'o_w_ukv': nrm((N_ODD, MLA_KV_LORA, MLA_HEADS * (MLA_NOPE + MLA_V)), MLA_KV_LORA ** -0.5),
        'o_gqa_q_norm': 1.0 + nrm((N_ODD, GQA_HEAD_DIM), 0.02),
        'o_gqa_k_norm': 1.0 + nrm((N_ODD, GQA_HEAD_DIM), 0.02),
        'moe_wg': nrm((DEPTH, dm, MOE_GROUPS), dm ** -0.5),
        'moe_bg': nrm((DEPTH, MOE_GROUPS), 0.01),
        'moe_we': nrm((DEPTH, dm, MOE_EXPERTS), dm ** -0.5),
        'moe_be': nrm((DEPTH, MOE_EXPERTS), 0.01),
        'moe_w1': nrm((DEPTH, MOE_EXPERTS, dm, MOE_HIDDEN), dm ** -0.5),
        'moe_w3': nrm((DEPTH, MOE_EXPERTS, dm, MOE_HIDDEN), dm ** -0.5),
        'moe_w2': nrm((DEPTH, MOE_EXPERTS, MOE_HIDDEN, dm), DEEPNORM_BETA * MOE_HIDDEN ** -0.5),
    }


def reference(x, c, ctx, c_ctx, ada_w, ada_b, ln_g, ln_b,
              e_w_in, e_w_out, e_conv_w, e_conv_b, e_dt_bias, e_a_log, e_d_skip, e_gnorm_w, e_rpb,
              o_w_in, o_w_out, o_mla_q_norm, o_mla_kv_norm, o_w_uq, o_w_ukv, o_gqa_q_norm, o_gqa_k_norm,
              moe_wg, moe_bg, moe_we, moe_be, moe_w1, moe_w3, moe_w2):
    rows = x.shape[1] // GRID_W
    s_c = jax.nn.silu(c)
    s_cc = jax.nn.silu(c_ctx)
    xc = ctx
    for layer in range(DEPTH):
        ctx_needed = layer < DEPTH - 1
        i = layer // 2
        mod = jnp.split((s_c @ ada_w[layer] + ada_b[layer])[:, None, :], N_MOD, axis=-1)
        mod_c = jnp.split(s_cc @ ada_w[layer] + ada_b[layer], N_MOD, axis=-1)
        h = modulate(x, mod[0], mod[1])
        hc = modulate(xc, mod_c[0], mod_c[1])
        if layer % 2 == 0:
            y, yc = even_mixer(h, hc, rows, e_w_in[i], e_w_out[i], e_conv_w[i], e_conv_b[i], e_dt_bias[i],
                               e_a_log[i], e_d_skip[i], e_gnorm_w[i], e_rpb[i])
        else:
            y, yc = odd_mixer(h, hc, o_w_in[i], o_w_out[i], o_mla_q_norm[i], o_mla_kv_norm[i], o_w_uq[i],
                              o_w_ukv[i], o_gqa_q_norm[i], o_gqa_k_norm[i], ctx_needed)
        x = layer_norm(DEEPNORM_ALPHA * x + mod[2] * y, ln_g[layer, 0], ln_b[layer, 0])
        h = modulate(x, mod[3], mod[4])
        if ctx_needed:
            xc = layer_norm(DEEPNORM_ALPHA * xc + mod_c[2] * yc, ln_g[layer, 0], ln_b[layer, 0])
            hc = modulate(xc, mod_c[3], mod_c[4])
            n_ctx = hc.shape[0] * hc.shape[1]
            tokens = jnp.concatenate([hc.reshape(n_ctx, D_MODEL), h.reshape(-1, D_MODEL)], 0)
            f = hier_moe(tokens, moe_wg[layer], moe_bg[layer], moe_we[layer], moe_be[layer],
                         moe_w1[layer], moe_w3[layer], moe_w2[layer])
            xc = layer_norm(DEEPNORM_ALPHA * xc + mod_c[5] * f[:n_ctx].reshape(hc.shape),
                            ln_g[layer, 1], ln_b[layer, 1])
            f = f[n_ctx:].reshape(h.shape)
        else:
            f = hier_moe(h.reshape(-1, D_MODEL), moe_wg[layer], moe_bg[layer], moe_we[layer], moe_be[layer],
                         moe_w1[layer], moe_w3[layer], moe_w2[layer]).reshape(h.shape)
        x = layer_norm(DEEPNORM_ALPHA * x + mod[5] * f, ln_g[layer, 1], ln_b[layer, 1])
    return x
```

```python
import functools
import math

import numpy as np
import jax
import jax.numpy as jnp
from jax import lax
from jax.experimental import pallas as pl
from jax.experimental.pallas import tpu as pltpu

F32 = jnp.float32
BF16 = jnp.bfloat16

D_MODEL = 1024
GRID_W = 64
NORM_EPS = 1e-6
ROPE_THETA = 10000.0
N_MOD = 6

SSD_HEADS = 16
SSD_HEAD_DIM = 64
SSD_INNER = 1024
SSD_GROUPS = 4
SSD_STATE = 128
SSD_BC = 512
XBC_DIM = 2048
SSD_CONV = 5
SSD_CHUNK = 128

NA_HEADS = 8
NA_HEAD_DIM = 64
NA_DIM = 512
NA_ROWS = 8
NA_COLS = 16

MLA_HEADS = 8
MLA_Q_LORA = 384
MLA_KV_LORA = 256
MLA_NOPE = 64
MLA_ROPE = 32
MLA_V = 64
MLA_SCALE = (MLA_NOPE + MLA_ROPE) ** -0.5
GQA_HEADS = 8
GQA_KV_HEADS = 2
GQA_HEAD_DIM = 64
GQA_SCALE = GQA_HEAD_DIM ** -0.5

MOE_GROUPS = 4
MOE_EPG = 8
MOE_EXPERTS = 32
MOE_TOPK = 2
MOE_HIDDEN = 512

LANE = 128
TM = 256
MOE_TM = 256
ATT_TQ = 256
MOD_ROWS = 24
NEG = -1e30
VMEM_LIMIT = 56 * 1024 * 1024


def _cp(*sem):
    return pltpu.CompilerParams(dimension_semantics=sem, vmem_limit_bytes=VMEM_LIMIT)


def _dot(a, b):
    return jnp.dot(a, b, preferred_element_type=F32)


def _dot_nt(a, b):
    return lax.dot_general(a, b, (((1,), (1,)), ((), ())), preferred_element_type=F32)


def _split3(x):
    x1 = x.astype(BF16)
    r = x - x1.astype(F32)
    x2 = r.astype(BF16)
    r = r - x2.astype(F32)
    return x1, x2, r.astype(BF16)


def _dot3_l(x, m):
    a, b, c = _split3(x)
    return _dot(a, m) + _dot(b, m) + _dot(c, m)


def _dot3_r(m, x):
    a, b, c = _split3(x)
    return _dot(m, a) + _dot(m, b) + _dot(m, c)


def _silu(x):
    return x * jax.nn.sigmoid(x)


def _softplus(x):
    return jnp.maximum(x, 0.0) + jnp.log1p(jnp.exp(-jnp.abs(x)))


def _layer_norm(v, g, b):
    mu = jnp.mean(v, -1, keepdims=True)
    vc = v - mu
    var = jnp.mean(vc * vc, -1, keepdims=True)
    return vc * lax.rsqrt(var + NORM_EPS) * g + b


class _Layout:
    def __init__(self, b, c, l):
        assert c % TM == 0 and l % TM == 0 and (b * c) % l == 0 and l % GRID_W == 0
        assert b + 1 <= MOD_ROWS
        self.b, self.c, self.l = b, c, l
        self.nc = b * c
        self.t = b * c + b * l
        self.nct = self.nc // TM
        self.nt = self.t // TM
        self.tpb = l // TM
        self.rows = l // GRID_W

    def mod_row(self, i):
        return jnp.where(i < self.nct, self.b, (i - self.nct) // self.tpb)


def _ada_kernel(c_ref, w_ref, b_ref, o_ref):
    s = _silu(c_ref[...]).astype(BF16)
    o_ref[0] = _dot(s, w_ref[0].astype(BF16)) + b_ref[0]


def ada_mod(cvec, ada_w, ada_b):
    depth, dm, n = ada_w.shape
    tn = 1536
    assert n % tn == 0
    return pl.pallas_call(
        _ada_kernel,
        grid=(depth, n // tn),
        in_specs=[pl.BlockSpec((MOD_ROWS, dm), lambda l, j: (0, 0)),
                  pl.BlockSpec((1, dm, tn), lambda l, j: (l, 0, j)),
                  pl.BlockSpec((1, 1, tn), lambda l, j: (l, 0, j))],
        out_specs=pl.BlockSpec((1, MOD_ROWS, tn), lambda l, j: (l, 0, j)),
        out_shape=jax.ShapeDtypeStruct((depth, MOD_ROWS, n), F32),
        compiler_params=_cp("arbitrary", "arbitrary"),
        name="ada_mod",
    )(cvec, ada_w, ada_b.reshape(depth, 1, n))


def _proj_kernel(*refs, splits, with_t):
    x_ref, sh_ref, sc_ref, w_ref = refs[:4]
    pos = 4
    wt_ref = None
    if with_t:
        wt_ref = refs[pos]
        pos += 1
    outs = refs[pos:]
    xm = (x_ref[...] * (1.0 + sc_ref[0]) + sh_ref[0]).astype(BF16)
    off = 0
    for o_ref, (n, _) in zip(outs, splits):
        for c0 in range(0, n, 512):
            cw = min(512, n - c0)
            o_ref[:, c0:c0 + cw] = _dot(xm, w_ref[:, off + c0:off + c0 + cw]).astype(o_ref.dtype)
        off += n
    if with_t:
        outs[len(splits)][...] = _dot_nt(wt_ref[...], xm)


def fused_proj(lay, x, shift, scale, w, splits, wt=None):
    ntot = sum(n for n, _ in splits)
    assert w.shape == (D_MODEL, ntot)
    mod_spec = pl.BlockSpec((1, 1, D_MODEL), lambda i: (lay.mod_row(i), 0, 0))
    in_specs = [pl.BlockSpec((TM, D_MODEL), lambda i: (i, 0)), mod_spec, mod_spec,
                pl.BlockSpec((D_MODEL, ntot), lambda i: (0, 0))]
    args = [x, shift, scale, w]
    out_specs = [pl.BlockSpec((TM, n), lambda i: (i, 0)) for n, _ in splits]
    out_shape = [jax.ShapeDtypeStruct((lay.t, n), dt) for n, dt in splits]
    if wt is not None:
        r = wt.shape[0]
        in_specs.append(pl.BlockSpec((r, D_MODEL), lambda i: (0, 0)))
        args.append(wt)
        out_specs.append(pl.BlockSpec((r, TM), lambda i: (0, i)))
        out_shape.append(jax.ShapeDtypeStruct((r, lay.t), F32))
    return pl.pallas_call(
        functools.partial(_proj_kernel, splits=tuple(splits), with_t=wt is not None),
        grid=(lay.nt,),
        in_specs=in_specs, out_specs=out_specs, out_shape=out_shape,
        compiler_params=_cp("parallel"),
        name="fused_proj",
    )(*args)


def _outproj_kernel(*refs, n_mix, alpha):
    x_ref = refs[0]
    mix = refs[1:1 + n_mix]
    ws = refs[1 + n_mix:1 + 2 * n_mix]
    gate_ref, g_ref, b_ref, sh_ref, sc_ref, wr_hi, wr_lo, br_ref = refs[1 + 2 * n_mix:9 + 2 * n_mix]
    xo_ref, h_ref, lg_ref = refs[9 + 2 * n_mix:]
    y = _dot(mix[0][...], ws[0][...])
    for a, w in zip(mix[1:], ws[1:]):
        y = y + _dot(a[...], w[...])
    xn = _layer_norm(alpha * x_ref[...] + gate_ref[0] * y, g_ref[...], b_ref[...])
    xo_ref[...] = xn
    h = xn * (1.0 + sc_ref[0]) + sh_ref[0]
    h_ref[...] = h
    h_hi = h.astype(BF16)
    h_lo = (h - h_hi.astype(F32)).astype(BF16)
    lg_ref[...] = _dot(h_hi, wr_hi[...]) + (_dot(h_lo, wr_hi[...]) + _dot(h_hi, wr_lo[...])) + br_ref[...]


def outproj_ln(lay, x, mixes, ws, gate, ln_g, ln_b, shift, scale, wr_hi, wr_lo, br, alpha):
    mod_spec = pl.BlockSpec((1, 1, D_MODEL), lambda i: (lay.mod_row(i), 0, 0))
    row_spec = pl.BlockSpec((1, D_MODEL), lambda i: (0, 0))
    in_specs = [pl.BlockSpec((TM, D_MODEL), lambda i: (i, 0))]
    in_specs += [pl.BlockSpec((TM, a.shape[1]), lambda i: (i, 0)) for a in mixes]
    in_specs += [pl.BlockSpec(w.shape, lambda i: (0, 0)) for w in ws]
    in_specs += [mod_spec, row_spec, row_spec, mod_spec, mod_spec,
                 pl.BlockSpec((D_MODEL, LANE), lambda i: (0, 0)), pl.BlockSpec((D_MODEL, LANE), lambda i: (0, 0)),
                 pl.BlockSpec((1, LANE), lambda i: (0, 0))]
    return pl.pallas_call(
        functools.partial(_outproj_kernel, n_mix=len(mixes), alpha=alpha),
        grid=(lay.nt,),
        in_specs=in_specs,
        out_specs=[pl.BlockSpec((TM, D_MODEL), lambda i: (i, 0)), pl.BlockSpec((TM, D_MODEL), lambda i: (i, 0)),
                   pl.BlockSpec((TM, LANE), lambda i: (i, 0))],
        out_shape=[jax.ShapeDtypeStruct((lay.t, D_MODEL), F32), jax.ShapeDtypeStruct((lay.t, D_MODEL), F32),
                   jax.ShapeDtypeStruct((lay.t, LANE), F32)],
        compiler_params=_cp("parallel"),
        name="outproj_ln",
    )(x, *mixes, *ws, gate, ln_g, ln_b, shift, scale, wr_hi, wr_lo, br)


def _na_kernel(q_ref, kl_ref, vl_ref, kc_ref, vc_ref, bt_ref, o_ref, *, ncq, rows):
    j = pl.program_id(1)
    q = q_ref[...]
    kc = kc_ref[...]
    vc = vc_ref[...]
    lane = lax.broadcasted_iota(jnp.int32, (1, NA_DIM), 1)
    scale = NA_HEAD_DIM ** -0.5

    def heads(local):
        acc = jnp.zeros((GRID_W, NA_DIM), F32)
        if local:
            r = j - ncq
            r0 = jnp.clip(r - NA_ROWS // 2, 0, rows - NA_ROWS)
            dr0 = r0 - r + (NA_ROWS - 1)
            start = pl.multiple_of(r0 * GRID_W, GRID_W)
            kb = kl_ref[pl.ds(start, NA_ROWS * GRID_W), :]
            vb = vl_ref[pl.ds(start, NA_ROWS * GRID_W), :]
        for h in range(NA_HEADS):
            hm = (lane >= h * NA_HEAD_DIM) & (lane < (h + 1) * NA_HEAD_DIM)
            qh = jnp.where(hm, q, jnp.zeros_like(q))
            s_c = _dot_nt(qh, kc) * scale
            m = jnp.max(s_c, -1, keepdims=True)
            if local:
                s_l = _dot_nt(qh, kb) * scale + bt_ref[h, dr0]
                m = jnp.maximum(m, jnp.max(s_l, -1, keepdims=True))
                p_l = jnp.exp(s_l - m)
            p_c = jnp.exp(s_c - m)
            den = jnp.sum(p_c, -1, keepdims=True)
            o = _dot(p_c.astype(BF16), vc)
            if local:
                den = den + jnp.sum(p_l, -1, keepdims=True)
                o = o + _dot(p_l.astype(BF16), vb)
            acc = jnp.where(hm, o / den, acc)
        o_ref[...] = acc.astype(o_ref.dtype)

    @pl.when(j < ncq)
    def _():
        heads(False)

    @pl.when(j >= ncq)
    def _():
        heads(True)


def na_attention(lay, qkv, bias_tab):
    b, c, l = lay.b, lay.c, lay.l
    ncq = c // GRID_W
    rows = lay.rows
    assert rows >= NA_ROWS
    qb = lambda bi, j: jnp.where(j < ncq, bi * ncq + j, lay.nc // GRID_W + bi * rows + (j - ncq))
    return pl.pallas_call(
        functools.partial(_na_kernel, ncq=ncq, rows=rows),
        grid=(b, ncq + rows),
        in_specs=[pl.BlockSpec((GRID_W, NA_DIM), lambda bi, j: (qb(bi, j), 0)),
                  pl.BlockSpec((l, NA_DIM), lambda bi, j: (lay.nc // l + bi, 1)),
                  pl.BlockSpec((l, NA_DIM), lambda bi, j: (lay.nc // l + bi, 2)),
                  pl.BlockSpec((c, NA_DIM), lambda bi, j: (bi, 1)),
                  pl.BlockSpec((c, NA_DIM), lambda bi, j: (bi, 2)),
                  pl.BlockSpec(bias_tab.shape, lambda bi, j: (0, 0, 0, 0))],
        out_specs=pl.BlockSpec((GRID_W, NA_DIM), lambda bi, j: (qb(bi, j), 0)),
        out_shape=jax.ShapeDtypeStruct((lay.t, NA_DIM), BF16),
        compiler_params=_cp("parallel", "arbitrary"),
        name="na_attention",
    )(qkv, qkv, qkv, qkv, qkv, bias_tab)


def na_bias_table(rpb):
    kw = NA_COLS
    col_start = np.clip(np.arange(GRID_W) - kw // 2, 0, GRID_W - kw)
    kc = np.arange(GRID_W)[None, :]
    jq = np.arange(GRID_W)[:, None]
    inside = (kc >= col_start[:, None]) & (kc < col_start[:, None] + kw)
    dc = np.clip(kc - jq + (NA_COLS - 1), 0, 2 * NA_COLS - 2)
    planes = jnp.where(inside[None, None], rpb[:, :, dc], NEG)
    tabs = []
    for dr0 in range(NA_ROWS):
        sel = planes[:, dr0:dr0 + NA_ROWS]
        tabs.append(jnp.transpose(sel, (0, 2, 1, 3)).reshape(NA_HEADS, GRID_W, NA_ROWS * GRID_W))
    return jnp.stack(tabs, 1).astype(F32)


def _conv_kernel(x_ref, p_ref, n_ref, w_ref, b_ref, o_ref, *, nct, tpc, tpl):
    i = pl.program_id(0)
    first = jnp.where(i < nct, i % tpc == 0, (i - nct) % tpl == 0)
    last = jnp.where(i < nct, i % tpc == tpc - 1, (i - nct) % tpl == tpl - 1)
    half = SSD_CONV // 2
    for c0 in range(0, XBC_DIM, 512):
        cs = slice(c0, c0 + 512)
        prev = jnp.where(first, 0.0, p_ref[:, cs])
        nxt = jnp.where(last, 0.0, n_ref[:, cs])
        ext = jnp.concatenate([prev, x_ref[:, cs], nxt], axis=0)
        n_ext = TM + 16
        acc = jnp.zeros((TM, 512), F32) + b_ref[:, cs]
        for k in range(SSD_CONV):
            shifted = ext if k == half else pltpu.roll(ext, (half - k) % n_ext, 0)
            acc = acc + w_ref[k:k + 1, cs] * shifted[8:8 + TM]
        o_ref[:, cs] = _silu(acc)


def conv_silu(lay, xbc, w8, bias):
    n8 = lay.t // 8
    return pl.pallas_call(
        functools.partial(_conv_kernel, nct=lay.nct, tpc=lay.c // TM, tpl=lay.tpb),
        grid=(lay.nt,),
        in_specs=[pl.BlockSpec((TM, XBC_DIM), lambda i: (i, 0)),
                  pl.BlockSpec((8, XBC_DIM), lambda i: (jnp.maximum(i * (TM // 8) - 1, 0), 0)),
                  pl.BlockSpec((8, XBC_DIM), lambda i: (jnp.minimum((i + 1) * (TM // 8), n8 - 1), 0)),
                  pl.BlockSpec((8, XBC_DIM), lambda i: (0, 0)),
                  pl.BlockSpec((1, XBC_DIM), lambda i: (0, 0))],
        out_specs=pl.BlockSpec((TM, XBC_DIM), lambda i: (i, 0)),
        out_shape=jax.ShapeDtypeStruct((lay.t, XBC_DIM), F32),
        compiler_params=_cp("parallel"),
        name="conv_silu",
    )(xbc, xbc, xbc, w8, bias)


def _ssd_kernel(*refs, rev):
    if rev:
        (u_ref, dt_ref, dtt_ref, bn_ref, an_ref, bc_ref, ac_ref, e_ref,
         yf_ref, z_ref, dsk_ref, gn_ref, o_ref, state, ybuf) = refs
    else:
        u_ref, dt_ref, dtt_ref, bn_ref, an_ref, bc_ref, ac_ref, e_ref, o_ref, state, ybuf = refs
    qn = SSD_CHUNK

    @pl.when(pl.program_id(1) == 0)
    def _():
        state[...] = jnp.zeros_like(state)

    xs = u_ref[:, :SSD_INNER]
    dt = _softplus(dt_ref[...] + bn_ref[...])
    a = dt * an_ref[...]
    dt_t = _softplus(dtt_ref[...] + bc_ref[...])
    a_t = dt_t * ac_ref[...]
    li = lax.broadcasted_iota(jnp.int32, (qn, qn), 0)
    si = lax.broadcasted_iota(jnp.int32, (qn, qn), 1)
    keep = (si >= li) if rev else (si <= li)
    tri = keep.astype(BF16)
    tri_t = ((li >= si) if rev else (li <= si)).astype(BF16)
    a_cs = _dot3_r(tri, a)
    a_cs_t = _dot3_l(a_t, tri_t)
    e = e_ref[...]
    dt_x = _dot3_l(dt, e)
    acs_x = _dot3_l(a_cs, e)
    end = 0 if rev else qn - 1
    acs_end = acs_x[end:end + 1, :]
    xdt = xs * dt_x
    xdt_b = xdt.astype(BF16)
    xdte = (xdt * jnp.exp(acs_end - acs_x)).astype(BF16)
    ea_x = jnp.exp(acs_x)
    cd_x = jnp.exp(acs_end)
    lane = lax.broadcasted_iota(jnp.int32, (1, LANE), 1)
    lo = lane < SSD_HEAD_DIM
    cb = None
    for p in range(SSD_HEADS // 2):
        g = p // 2
        ps = slice(p * LANE, (p + 1) * LANE)
        if p % 2 == 0:
            bg = u_ref[:, SSD_INNER + g * SSD_STATE:SSD_INNER + (g + 1) * SSD_STATE]
            cg = u_ref[:, SSD_INNER + SSD_BC + g * SSD_STATE:SSD_INNER + SSD_BC + (g + 1) * SSD_STATE].astype(BF16)
            cb = _dot_nt(cg, bg.astype(BF16))
            bg_t = bg.T.astype(BF16)
        xp = xdt_b[:, ps]
        y = None
        for hh in range(2):
            h = 2 * p + hh
            seg = a_cs[:, h:h + 1] - a_cs_t[h:h + 1, :]
            m = (jnp.exp(jnp.where(keep, seg, NEG)) * cb).astype(BF16)
            xh = jnp.where(lo if hh == 0 else jnp.logical_not(lo), xp, jnp.zeros_like(xp))
            yh = _dot(m, xh)
            y = yh if y is None else y + yh
        st = state[p]
        y = y + _dot(cg, st.astype(BF16)) * ea_x[:, ps]
        state[p] = cd_x[:, ps] * st + _dot(bg_t, xdte[:, ps])
        ybuf[:, ps] = y
    if rev:
        y = dsk_ref[...] * xs + yf_ref[...] + ybuf[...]
        y = y * _silu(z_ref[...])
        o_ref[...] = (y * lax.rsqrt(jnp.mean(y * y, -1, keepdims=True) + NORM_EPS) * gn_ref[...]).astype(o_ref.dtype)
    else:
        o_ref[...] = ybuf[...]


def ssd_pass(lay, rev, u, dt, dtt, bias_nat, a_nat, bias_col, a_col, expand, extra=None):
    qn = SSD_CHUNK
    ncc, ncl = lay.c // qn, lay.l // qn
    nctq = lay.nc // qn
    d = 1 if rev else 0

    def chunk(bi, j):
        jc = (ncc - 1 - j) if rev else j
        jl = (ncl - 1 - (j - ncc)) if rev else (j - ncc)
        return jnp.where(j < ncc, bi * ncc + jc, nctq + bi * ncl + jl)

    row = lambda n: pl.BlockSpec((1, n), lambda bi, j: (0, 0))
    in_specs = [pl.BlockSpec((qn, XBC_DIM), lambda bi, j: (chunk(bi, j), 0)),
                pl.BlockSpec((qn, LANE), lambda bi, j: (chunk(bi, j), d)),
                pl.BlockSpec((SSD_HEADS, qn), lambda bi, j: (d, chunk(bi, j))),
                row(LANE), row(LANE),
                pl.BlockSpec((SSD_HEADS, 1), lambda bi, j: (0, 0)), pl.BlockSpec((SSD_HEADS, 1), lambda bi, j: (0, 0)),
                pl.BlockSpec((LANE, SSD_INNER), lambda bi, j: (0, 0))]
    args = [u, dt, dtt, bias_nat, a_nat, bias_col, a_col, expand]
    if rev:
        yf, z, dsk, gn = extra
        in_specs += [pl.BlockSpec((qn, SSD_INNER), lambda bi, j: (chunk(bi, j), 0)),
                     pl.BlockSpec((qn, SSD_INNER), lambda bi, j: (chunk(bi, j), 0)),
                     row(SSD_INNER), row(SSD_INNER)]
        args += [yf, z, dsk, gn]
    return pl.pallas_call(
        functools.partial(_ssd_kernel, rev=rev),
        grid=(lay.b, ncc + ncl),
        in_specs=in_specs,
        out_specs=pl.BlockSpec((qn, SSD_INNER), lambda bi, j: (chunk(bi, j), 0)),
        out_shape=jax.ShapeDtypeStruct((lay.t, SSD_INNER), BF16 if rev else F32),
        scratch_shapes=[pltpu.VMEM((SSD_HEADS // 2, SSD_STATE, LANE), F32), pltpu.VMEM((qn, SSD_INNER), F32)],
        compiler_params=_cp("parallel", "arbitrary"),
        name="ssd_bwd" if rev else "ssd_fwd",
    )(*args)


def _rope_tables(lay):
    t = np.arange(lay.l)
    pos_r, pos_c = t // GRID_W, t % GRID_W

    def half_tables(half):
        inv = ROPE_THETA ** (-np.arange(0, half, 2, dtype=np.float32) / half)
        cos, sin = [], []
        for pos in (pos_r, pos_c):
            ang = pos.astype(np.float32)[:, None] * inv[None, :]
            cos += [np.cos(ang), np.cos(ang)]
            sin += [-np.sin(ang), np.sin(ang)]
        return np.concatenate(cos, 1), np.concatenate(sin, 1)

    c32, s32 = half_tables(MLA_ROPE // 2)
    c64, s64 = half_tables(GQA_HEAD_DIM // 2)
    mla_c = np.concatenate([np.ones((lay.l, MLA_NOPE)), c32, np.zeros((lay.l, LANE - MLA_NOPE - MLA_ROPE))], 1)
    mla_s = np.concatenate([np.zeros((lay.l, MLA_NOPE)), s32, np.zeros((lay.l, LANE - MLA_NOPE - MLA_ROPE))], 1)
    ident_c = np.concatenate([np.ones((TM, MLA_NOPE + MLA_ROPE)), np.zeros((TM, LANE - MLA_NOPE - MLA_ROPE))], 1)
    gqa_c = np.concatenate([c64, c64], 1)
    gqa_s = np.concatenate([s64, s64], 1)
    f = lambda ident, tab: jnp.asarray(np.concatenate([ident, tab], 0), F32)
    return (f(ident_c, mla_c), f(np.zeros((TM, LANE)), mla_s),
            f(np.ones((TM, LANE)), gqa_c), f(np.zeros((TM, LANE)), gqa_s))


def _prep_kernel(cq_ref, ckv_ref, gq_ref, gqp_ref, gk_ref, gkp_ref, kra_ref, krb_ref,
                 qn_ref, kvn_ref, gqn_ref, gqnp_ref, gkn_ref, gknp_ref,
                 wq_ref, wqp_ref, wk_ref, wv_ref, bd_ref, mc_ref, ms_ref, gc_ref, gs_ref,
                 mq_ref, mk_ref, mv_ref, oq_ref, ok_ref):
    def rms(x, g):
        return x * lax.rsqrt(jnp.mean(x * x, -1, keepdims=True) + NORM_EPS) * g

    mc, ms = mc_ref[...], ms_ref[...]
    mc8 = jnp.concatenate([mc] * MLA_HEADS, 1)
    ms8 = jnp.concatenate([ms] * MLA_HEADS, 1)
    rq = rms(cq_ref[...], qn_ref[...]).astype(BF16)
    q = _dot(rq, wq_ref[...]) * mc8 + _dot(rq, wqp_ref[...]) * ms8
    mq_ref[...] = (q * MLA_SCALE).astype(BF16)
    rkv = rms(ckv_ref[...], kvn_ref[...]).astype(BF16)
    kr = kra_ref[...] * mc + krb_ref[...] * ms
    mk_ref[...] = (_dot(rkv, wk_ref[...]) + jnp.concatenate([kr] * MLA_HEADS, 1)).astype(BF16)
    mv_ref[...] = _dot(rkv, wv_ref[...]).astype(BF16)
    gc4 = jnp.concatenate([gc_ref[...]] * 4, 1)
    gs4 = jnp.concatenate([gs_ref[...]] * 4, 1)
    bd = bd_ref[...]

    def qk_norm_rope(u, up, g, gp, scale):
        sq = u * u
        hi = sq.astype(BF16)
        lo = (sq - hi.astype(F32)).astype(BF16)
        r = lax.rsqrt(_dot(hi, bd) + _dot(lo, bd) + NORM_EPS)
        return ((u * g * gc4 + up * gp * gs4) * (r * scale)).astype(BF16)

    oq_ref[...] = qk_norm_rope(gq_ref[...], gqp_ref[...], gqn_ref[...], gqnp_ref[...], GQA_SCALE)
    ok_ref[...] = qk_norm_rope(gk_ref[...], gkp_ref[...], gkn_ref[...], gknp_ref[...], 1.0)


def odd_prep(lay, acts, norms, weights, tables):
    tab_row = lambda i: jnp.where(i < lay.nct, 0, 1 + (i - lay.nct) % lay.tpb)
    tok = lambda n: pl.BlockSpec((TM, n), lambda i: (i, 0))
    const = lambda a: pl.BlockSpec(a.shape, lambda i: (0, 0))
    in_specs = [tok(a.shape[1]) for a in acts] + [const(a) for a in norms] + [const(a) for a in weights]
    in_specs += [pl.BlockSpec((TM, LANE), lambda i: (tab_row(i), 0)) for _ in tables]
    widths = (D_MODEL, D_MODEL, D_MODEL, 512, 512)
    return pl.pallas_call(
        _prep_kernel,
        grid=(lay.nt,),
        in_specs=in_specs,
        out_specs=[tok(n) for n in widths],
        out_shape=[jax.ShapeDtypeStruct((lay.t, n), BF16) for n in widths],
        compiler_params=_cp("parallel"),
        name="odd_prep",
    )(*acts, *norms, *weights, *tables)


def _attn_kernel(q_ref, kc_ref, kl_ref, vc_ref, vl_ref, o_ref, *, ncq, q_shared):
    j = pl.program_id(2)

    def run(with_lat):
        out = None
        for hh in range(2):
            q = q_ref[...] if q_shared else q_ref[:, hh * LANE:(hh + 1) * LANE]
            hs = slice(hh * LANE, (hh + 1) * LANE)
            s_c = _dot_nt(q, kc_ref[:, hs])
            m = jnp.max(s_c, -1, keepdims=True)
            if with_lat:
                s_l = _dot_nt(q, kl_ref[:, hs])
                m = jnp.maximum(m, jnp.max(s_l, -1, keepdims=True))
                p_l = jnp.exp(s_l - m)
            p_c = jnp.exp(s_c - m)
            den = jnp.sum(p_c, -1, keepdims=True)
            o = _dot(p_c.astype(BF16), vc_ref[:, hs])
            if with_lat:
                den = den + jnp.sum(p_l, -1, keepdims=True)
                o = o + _dot(p_l.astype(BF16), vl_ref[:, hs])
            o = o / den
            out = o if out is None else out + o
        o_ref[...] = out.astype(o_ref.dtype)

    @pl.when(j < ncq)
    def _():
        run(False)

    @pl.when(j >= ncq)
    def _():
        run(True)


def dense_attention(lay, q, k, v, q_shared, kv_block):
    b, c, l = lay.b, lay.c, lay.l
    tq = ATT_TQ
    ncq, nlq = c // tq, l // tq
    qw = LANE if q_shared else 2 * LANE
    qb = lambda bi, j: jnp.where(j < ncq, bi * ncq + j, lay.nc // tq + bi * nlq + (j - ncq))
    ctx_spec = pl.BlockSpec((c, 2 * LANE), lambda bi, p, j: (bi, kv_block(p)))
    lat_spec = pl.BlockSpec((l, 2 * LANE), lambda bi, p, j: (lay.nc // l + bi, kv_block(p)))
    return pl.pallas_call(
        functools.partial(_attn_kernel, ncq=ncq, q_shared=q_shared),
        grid=(b, 4, ncq + nlq),
        in_specs=[pl.BlockSpec((tq, qw), lambda bi, p, j: (qb(bi, j), p)), ctx_spec, lat_spec, ctx_spec, lat_spec],
        out_specs=pl.BlockSpec((tq, LANE), lambda bi, p, j: (qb(bi, j), p)),
        out_shape=jax.ShapeDtypeStruct((lay.t, 4 * LANE), BF16),
        compiler_params=_cp("parallel", "parallel", "arbitrary"),
        name="dense_attention",
    )(q, k, k, v, v)


def _moe_ffn_kernel(be_ref, nb_ref, tok_hbm, h_hbm, sw_ref, w1_ref, w3_ref, w2_ref, y_ref, idx, xbuf, sem_i, sem_g):
    i = pl.program_id(0)

    @pl.when(i < nb_ref[0])
    def _():
        icp = pltpu.make_async_copy(tok_hbm.at[pl.ds(i * MOE_TM, MOE_TM)], idx, sem_i)
        icp.start()
        icp.wait()

        def issue(r, carry):
            pltpu.make_async_copy(h_hbm.at[pl.ds(idx[r], 1), :], xbuf.at[pl.ds(r, 1), :], sem_g).start()
            return carry

        def drain(r, carry):
            pltpu.make_async_copy(h_hbm.at[pl.ds(0, 1), :], xbuf.at[pl.ds(r, 1), :], sem_g).wait()
            return carry

        lax.fori_loop(0, MOE_TM, issue, 0)
        lax.fori_loop(0, MOE_TM, drain, 0)
        x = xbuf[...].astype(BF16)
        hid = (_silu(_dot(x, w1_ref[0])) * _dot(x, w3_ref[0])).astype(BF16)
        y_ref[...] = _dot(hid, w2_ref[0]) * sw_ref[...]

    @pl.when(i >= nb_ref[0])
    def _():
        y_ref[...] = jnp.zeros_like(y_ref)


def moe_ffn(h, blk_e, n_used, slot_tok, slot_w, w1, w3, w2):
    nb = blk_e.shape[0]
    wspec = lambda shape: pl.BlockSpec((1,) + shape, lambda i, be, nu: (be[i], 0, 0))
    return pl.pallas_call(
        _moe_ffn_kernel,
        grid_spec=pltpu.PrefetchScalarGridSpec(
            num_scalar_prefetch=2,
            grid=(nb,),
            in_specs=[pl.BlockSpec(memory_space=pl.ANY), pl.BlockSpec(memory_space=pl.ANY),
                      pl.BlockSpec((MOE_TM, 1), lambda i, be, nu: (i, 0)),
                      wspec((D_MODEL, MOE_HIDDEN)), wspec((D_MODEL, MOE_HIDDEN)), wspec((MOE_HIDDEN, D_MODEL))],
            out_specs=pl.BlockSpec((MOE_TM, D_MODEL), lambda i, be, nu: (i, 0)),
            scratch_shapes=[pltpu.SMEM((MOE_TM,), jnp.int32), pltpu.VMEM((MOE_TM, D_MODEL), F32),
                            pltpu.SemaphoreType.DMA, pltpu.SemaphoreType.DMA]),
        out_shape=jax.ShapeDtypeStruct((nb * MOE_TM, D_MODEL), F32),
        compiler_params=_cp("arbitrary"),
        name="moe_ffn",
    )(blk_e, n_used, slot_tok, h, slot_w, w1, w3, w2)


def _combine_kernel(pos_hbm, y_hbm, x_ref, gate_ref, g_ref, b_ref, o_ref, idx, buf, sem_i, sem_g, *, alpha):
    i = pl.program_id(0)
    icp = pltpu.make_async_copy(pos_hbm.at[pl.ds(i * (MOE_TOPK * TM), MOE_TOPK * TM)], idx, sem_i)
    icp.start()
    icp.wait()

    def issue(r, carry):
        for k in range(MOE_TOPK):
            pltpu.make_async_copy(y_hbm.at[pl.ds(idx[MOE_TOPK * r + k], 1), :], buf.at[k, pl.ds(r, 1), :], sem_g).start()
        return carry

    def drain(r, carry):
        for k in range(MOE_TOPK):
            pltpu.make_async_copy(y_hbm.at[pl.ds(0, 1), :], buf.at[k, pl.ds(r, 1), :], sem_g).wait()
        return carry

    lax.fori_loop(0, TM, issue, 0)
    lax.fori_loop(0, TM, drain, 0)
    f = buf[0] + buf[1]
    o_ref[...] = _layer_norm(alpha * x_ref[...] + gate_ref[0] * f, g_ref[...], b_ref[...])


def moe_combine_ln(lay, pos, y, x, gate, ln_g, ln_b, alpha):
    row_spec = pl.BlockSpec((1, D_MODEL), lambda i: (0, 0))
    return pl.pallas_call(
        functools.partial(_combine_kernel, alpha=alpha),
        grid=(lay.nt,),
        in_specs=[pl.BlockSpec(memory_space=pl.ANY), pl.BlockSpec(memory_space=pl.ANY),
                  pl.BlockSpec((TM, D_MODEL), lambda i: (i, 0)),
                  pl.BlockSpec((1, 1, D_MODEL), lambda i: (lay.mod_row(i), 0, 0)), row_spec, row_spec],
        out_specs=pl.BlockSpec((TM, D_MODEL), lambda i: (i, 0)),
        out_shape=jax.ShapeDtypeStruct((lay.t, D_MODEL), F32),
        scratch_shapes=[pltpu.SMEM((MOE_TOPK * TM,), jnp.int32), pltpu.VMEM((MOE_TOPK, TM, D_MODEL), F32),
                        pltpu.SemaphoreType.DMA, pltpu.SemaphoreType.DMA],
        compiler_params=_cp("arbitrary"),
        name="moe_combine_ln",
    )(pos, y, x, gate, ln_g, ln_b)


def moe_route(logits):
    t = logits.shape[0]
    g_prob = jax.nn.softmax(logits[:, :MOE_GROUPS], axis=-1)
    g_gate, g_sel = lax.top_k(g_prob, 1)
    e_logits = logits[:, MOE_GROUPS:MOE_GROUPS + MOE_EXPERTS].reshape(t, MOE_GROUPS, MOE_EPG)
    e_in = jnp.take_along_axis(e_logits, g_sel[:, :, None], axis=1)[:, 0]
    e_top, e_idx = lax.top_k(e_in, MOE_TOPK)
    gate = g_gate * jax.nn.softmax(e_top, axis=-1)
    expert = g_sel * MOE_EPG + e_idx
    return expert.astype(jnp.int32), gate


def moe_dispatch_plan(expert, gate):
    t = expert.shape[0]
    n_assign = t * MOE_TOPK
    flat_e = expert.reshape(n_assign)
    flat_tok = jnp.arange(n_assign, dtype=jnp.int32) // MOE_TOPK
    order = jnp.argsort(flat_e)
    se = flat_e[order]
    counts = jnp.bincount(flat_e, length=MOE_EXPERTS)
    starts = jnp.cumsum(counts) - counts
    pcounts = (counts + MOE_TM - 1) // MOE_TM * MOE_TM
    pends = jnp.cumsum(pcounts)
    pstarts = pends - pcounts
    dest = (pstarts[se] + (jnp.arange(n_assign, dtype=jnp.int32) - starts[se])).astype(jnp.int32)
    nb = (n_assign + MOE_EXPERTS * (MOE_TM - 1)) // MOE_TM
    slot_tok = jnp.zeros((nb * MOE_TM,), jnp.int32).at[dest].set(flat_tok[order])
    slot_w = jnp.zeros((nb * MOE_TM,), F32).at[dest].set(gate.reshape(n_assign)[order])
    blk_e = jnp.minimum(jnp.searchsorted(pends, jnp.arange(nb, dtype=jnp.int32) * MOE_TM, side='right'),
                        MOE_EXPERTS - 1).astype(jnp.int32)
    n_used = (pends[-1] // MOE_TM).astype(jnp.int32).reshape(1)
    pos = jnp.zeros((n_assign,), jnp.int32).at[order].set(dest)
    return blk_e, n_used, slot_tok, slot_w.reshape(-1, 1), pos


def _rope_partner(dim):
    q = dim // 4
    idx = np.arange(dim)
    return np.where((idx // q) % 2 == 0, idx + q, idx - q)


def _even_weights(w_in, w_out, conv_w, conv_b, dt_bias, a_log, d_skip, gnorm_w):
    n_main = NA_DIM * 3 + SSD_INNER + XBC_DIM
    w_dt = jnp.zeros((D_MODEL, 2 * LANE), F32)
    w_dt = w_dt.at[:, :SSD_HEADS].set(w_in[:, n_main:n_main + SSD_HEADS])
    w_dt = w_dt.at[:, LANE:LANE + SSD_HEADS].set(w_in[:, n_main + SSD_HEADS:])
    w = jnp.concatenate([w_in[:, :n_main], w_dt], 1).astype(BF16)
    wt = w_in[:, n_main:].T.astype(BF16)
    pad = lambda v: jnp.zeros((1, LANE), F32).at[0, :SSD_HEADS].set(v)
    a = -jnp.exp(a_log.astype(F32))
    expand = np.zeros((LANE, SSD_INNER), np.float32)
    for h in range(SSD_HEADS):
        expand[h, h * SSD_HEAD_DIM:(h + 1) * SSD_HEAD_DIM] = 1.0
    return dict(
        w=w, wt=wt,
        conv_w=jnp.zeros((8, XBC_DIM), F32).at[:SSD_CONV].set(conv_w), conv_b=conv_b.reshape(1, XBC_DIM),
        bias_nat=[pad(dt_bias[d]) for d in range(2)], a_nat=[pad(a[d]) for d in range(2)],
        bias_col=[dt_bias[d].reshape(SSD_HEADS, 1) for d in range(2)], a_col=[a[d].reshape(SSD_HEADS, 1) for d in range(2)],
        expand=jnp.asarray(expand, BF16),
        dskip=jnp.repeat(d_skip, SSD_HEAD_DIM).reshape(1, SSD_INNER), gnorm=gnorm_w.reshape(1, SSD_INNER),
        w_out=[w_out[:NA_DIM].astype(BF16), w_out[NA_DIM:].astype(BF16)])


def _odd_weights(w_in, w_out, q_norm, kv_norm, w_uq, w_ukv, gq_norm, gk_norm):
    o = np.cumsum([0, MLA_Q_LORA, GQA_HEADS * GQA_HEAD_DIM, MLA_KV_LORA, MLA_ROPE, 128, 128])
    w_cq, w_gq, w_ckv, w_kr, w_gk, w_gv = [w_in[:, o[i]:o[i + 1]] for i in range(6)]
    p64, p32 = _rope_partner(GQA_HEAD_DIM), _rope_partner(MLA_ROPE)
    head_perm = lambda n: np.concatenate([h * GQA_HEAD_DIM + p64 for h in range(n)])
    z64 = jnp.zeros((D_MODEL, GQA_HEAD_DIM), F32)

    def placed(wm):
        out = []
        for g in range(GQA_KV_HEADS):
            wg = wm[:, g * GQA_HEAD_DIM:(g + 1) * GQA_HEAD_DIM]
            out += [wg, z64, z64, wg]
        return jnp.concatenate(out, 1)

    kr_slab = lambda wm: jnp.zeros((D_MODEL, LANE), F32).at[:, MLA_NOPE:MLA_NOPE + MLA_ROPE].set(wm)
    w = jnp.concatenate([w_cq, w_ckv, w_gq, w_gq[:, head_perm(GQA_HEADS)],
                         placed(w_gk), placed(w_gk[:, head_perm(GQA_KV_HEADS)]),
                         kr_slab(w_kr), kr_slab(w_kr[:, p32]), placed(w_gv)], 1).astype(BF16)
    splits = [(MLA_Q_LORA, F32), (MLA_KV_LORA, F32), (512, F32), (512, F32), (512, F32), (512, F32),
              (LANE, F32), (LANE, F32), (512, BF16)]
    uq = w_uq.reshape(MLA_Q_LORA, MLA_HEADS, MLA_NOPE + MLA_ROPE)
    pad_q = jnp.zeros((MLA_Q_LORA, MLA_HEADS, LANE - MLA_NOPE - MLA_ROPE), F32)
    wq = jnp.concatenate([uq, pad_q], 2).reshape(MLA_Q_LORA, D_MODEL)
    uq_par = jnp.concatenate([jnp.zeros_like(uq[:, :, :MLA_NOPE]), uq[:, :, MLA_NOPE:][:, :, p32]], 2)
    wqp = jnp.concatenate([uq_par, pad_q], 2).reshape(MLA_Q_LORA, D_MODEL)
    ukv = w_ukv.reshape(MLA_KV_LORA, MLA_HEADS, MLA_NOPE + MLA_V)
    zk = jnp.zeros((MLA_KV_LORA, MLA_HEADS, MLA_NOPE), F32)
    wk = jnp.concatenate([ukv[:, :, :MLA_NOPE], zk], 2).reshape(MLA_KV_LORA, D_MODEL)
    uv = ukv[:, :, MLA_NOPE:]
    even = (np.arange(MLA_HEADS) % 2 == 0)[None, :, None]
    wv = jnp.concatenate([jnp.where(even, uv, 0.0), jnp.where(even, 0.0, uv)], 2).reshape(MLA_KV_LORA, D_MODEL)
    bd = np.kron(np.eye(8, dtype=np.float32), np.full((64, 64), 1.0 / 64, np.float32))
    tile = lambda v, n: jnp.tile(v, n).reshape(1, -1)
    return dict(
        w=w, splits=splits,
        norms=[q_norm.reshape(1, -1), kv_norm.reshape(1, -1), tile(gq_norm, 8), tile(gq_norm[p64], 8),
               tile(gk_norm, 8), tile(gk_norm[p64], 8)],
        weights=[wq.astype(BF16), wqp.astype(BF16), wk.astype(BF16), wv.astype(BF16), jnp.asarray(bd, BF16)],
        w_out=[w_out[:512].astype(BF16), w_out[512:].astype(BF16)])


def kernel(x, c, ctx, c_ctx, ada_w, ada_b, ln_g, ln_b, e_w_in, e_w_out, e_conv_w, e_conv_b, e_dt_bias, e_a_log, e_d_skip, e_gnorm_w, e_rpb, o_w_in, o_w_out, o_mla_q_norm, o_mla_kv_norm, o_w_uq, o_w_ukv, o_gqa_q_norm, o_gqa_k_norm, moe_wg, moe_bg, moe_we, moe_be, moe_w1, moe_w3, moe_w2):
    b, l, dm = x.shape
    cl = ctx.shape[1]
    depth = ada_w.shape[0]
    lay = _Layout(b, cl, l)
    alpha = (2.0 * depth) ** 0.25
    xa = jnp.concatenate([ctx.reshape(b * cl, dm), x.reshape(b * l, dm)], 0)
    cvec = jnp.zeros((MOD_ROWS, dm), F32).at[:b].set(c).at[b].set(c_ctx)
    mods = ada_mod(cvec, ada_w, ada_b).reshape(depth, MOD_ROWS, N_MOD, 1, dm).transpose(0, 2, 1, 3, 4)
    rope_tabs = _rope_tables(lay)
    row = lambda v: v.reshape(1, -1)
    for layer in range(depth):
        i = layer // 2
        mod = [mods[layer, k] for k in range(N_MOD)]
        if layer % 2 == 0:
            ew = _even_weights(e_w_in[i], e_w_out[i], e_conv_w[i], e_conv_b[i], e_dt_bias[i], e_a_log[i],
                               e_d_skip[i], e_gnorm_w[i])
            qkv, z, xbc, dt, dtt = fused_proj(
                lay, xa, mod[0], mod[1], ew['w'],
                [(3 * NA_DIM, BF16), (SSD_INNER, F32), (XBC_DIM, F32), (2 * LANE, F32)], wt=ew['wt'])
            na = na_attention(lay, qkv, na_bias_table(e_rpb[i]))
            u = conv_silu(lay, xbc, ew['conv_w'], ew['conv_b'])
            yf = ssd_pass(lay, False, u, dt, dtt, ew['bias_nat'][0], ew['a_nat'][0], ew['bias_col'][0],
                          ew['a_col'][0], ew['expand'])
            ssd = ssd_pass(lay, True, u, dt, dtt, ew['bias_nat'][1], ew['a_nat'][1], ew['bias_col'][1],
                           ew['a_col'][1], ew['expand'], extra=(yf, z, ew['dskip'], ew['gnorm']))
            mixes, w_outs = [na, ssd], ew['w_out']
        else:
            ow = _odd_weights(o_w_in[i], o_w_out[i], o_mla_q_norm[i], o_mla_kv_norm[i], o_w_uq[i], o_w_ukv[i],
                              o_gqa_q_norm[i], o_gqa_k_norm[i])
            cq, ckv, gq, gqp, gk, gkp, kra, krb, gv = fused_proj(lay, xa, mod[0], mod[1], ow['w'], ow['splits'])
            mq, mk, mv, gqr, gkr = odd_prep(lay, [cq, ckv, gq, gqp, gk, gkp, kra, krb], ow['norms'], ow['weights'],
                                            rope_tabs)
            o_mla = dense_attention(lay, mq, mk, mv, False, lambda p: p)
            o_gqa = dense_attention(lay, gqr, gkr, gv, True, lambda p: p // 2)
            mixes, w_outs = [o_mla, o_gqa], ow['w_out']
        wr = jnp.zeros((dm, LANE), F32).at[:, :MOE_GROUPS].set(moe_wg[layer])
        wr = wr.at[:, MOE_GROUPS:MOE_GROUPS + MOE_EXPERTS].set(moe_we[layer])
        wr_hi = wr.astype(BF16)
        wr_lo = (wr - wr_hi.astype(F32)).astype(BF16)
        br = jnp.zeros((1, LANE), F32).at[0, :MOE_GROUPS].set(moe_bg[layer])
        br = br.at[0, MOE_GROUPS:MOE_GROUPS + MOE_EXPERTS].set(moe_be[layer])
        xa, h2, logits = outproj_ln(lay, xa, mixes, w_outs, mod[2], row(ln_g[layer, 0]), row(ln_b[layer, 0]),
                                    mod[3], mod[4], wr_hi, wr_lo, br, alpha)
        expert, gate = moe_route(logits)
        blk_e, n_used, slot_tok, slot_w, pos = moe_dispatch_plan(expert, gate)
        y = moe_ffn(h2, blk_e, n_used, slot_tok, slot_w, moe_w1[layer].astype(BF16), moe_w3[layer].astype(BF16),
                    moe_w2[layer].astype(BF16))
        xa = moe_combine_ln(lay, pos, y, xa, mod[5], row(ln_g[layer, 1]), row(ln_b[layer, 1]), alpha)
    return xa[lay.nc:].reshape(b, l, dm)
```

```python
import functools
import math

import numpy as np
import jax
import jax.numpy as jnp
from jax import lax
from jax.experimental import pallas as pl
from jax.experimental.pallas import tpu as pltpu

F32 = jnp.float32
BF16 = jnp.bfloat16

D_MODEL = 1024
GRID_W = 64
NORM_EPS = 1e-6
ROPE_THETA = 10000.0
N_MOD = 6

SSD_HEADS = 16
SSD_HEAD_DIM = 64
SSD_INNER = 1024
SSD_GROUPS = 4
SSD_STATE = 128
SSD_BC = 512
XBC_DIM = 2048
SSD_CONV = 5
SSD_CHUNK = 128

NA_HEADS = 8
NA_HEAD_DIM = 64
NA_DIM = 512
NA_ROWS = 8
NA_COLS = 16

MLA_HEADS = 8
MLA_Q_LORA = 384
MLA_KV_LORA = 256
MLA_NOPE = 64
MLA_ROPE = 32
MLA_V = 64
MLA_SCALE = (MLA_NOPE + MLA_ROPE) ** -0.5
GQA_HEADS = 8
GQA_KV_HEADS = 2
GQA_HEAD_DIM = 64
GQA_SCALE = GQA_HEAD_DIM ** -0.5

MOE_GROUPS = 4
MOE_EPG = 8
MOE_EXPERTS = 32
MOE_TOPK = 2
MOE_HIDDEN = 512

LANE = 128
TM = 256
MOE_TM = 256
ATT_TQ = 256
MOD_ROWS = 24
NEG = -1e30
VMEM_LIMIT = 56 * 1024 * 1024


def _cp(*sem):
    return pltpu.CompilerParams(dimension_semantics=sem, vmem_limit_bytes=VMEM_LIMIT)


def _dot(a, b):
    return jnp.dot(a, b, preferred_element_type=F32)


def _dot_nt(a, b):
    return lax.dot_general(a, b, (((1,), (1,)), ((), ())), preferred_element_type=F32)


def _split3(x):
    x1 = x.astype(BF16)
    r = x - x1.astype(F32)
    x2 = r.astype(BF16)
    r = r - x2.astype(F32)
    return x1, x2, r.astype(BF16)


def _dot3_l(x, m):
    a, b, c = _split3(x)
    return _dot(a, m) + _dot(b, m) + _dot(c, m)


def _dot3_r(m, x):
    a, b, c = _split3(x)
    return _dot(m, a) + _dot(m, b) + _dot(m, c)


def _silu(x):
    return x * jax.nn.sigmoid(x)


def _softplus(x):
    return jnp.maximum(x, 0.0) + jnp.log1p(jnp.exp(-jnp.abs(x)))


def _layer_norm(v, g, b):
    mu = jnp.mean(v, -1, keepdims=True)
    vc = v - mu
    var = jnp.mean(vc * vc, -1, keepdims=True)
    return vc * lax.rsqrt(var + NORM_EPS) * g + b


class _Layout:
    def __init__(self, b, c, l):
        assert c % TM == 0 and l % TM == 0 and (b * c) % l == 0 and l % GRID_W == 0
        assert b + 1 <= MOD_ROWS
        self.b, self.c, self.l = b, c, l
        self.nc = b * c
        self.t = b * c + b * l
        self.nct = self.nc // TM
        self.nt = self.t // TM
        self.tpb = l // TM
        self.rows = l // GRID_W

    def mod_row(self, i):
        return jnp.where(i < self.nct, self.b, (i - self.nct) // self.tpb)


def _ada_kernel(c_ref, w_ref, b_ref, o_ref):
    s = _silu(c_ref[...]).astype(BF16)
    o_ref[0] = _dot(s, w_ref[0].astype(BF16)) + b_ref[0]


def ada_mod(cvec, ada_w, ada_b):
    depth, dm, n = ada_w.shape
    tn = 1536
    assert n % tn == 0
    return pl.pallas_call(
        _ada_kernel,
        grid=(depth, n // tn),
        in_specs=[pl.BlockSpec((MOD_ROWS, dm), lambda l, j: (0, 0)),
                  pl.BlockSpec((1, dm, tn), lambda l, j: (l, 0, j)),
                  pl.BlockSpec((1, 1, tn), lambda l, j: (l, 0, j))],
        out_specs=pl.BlockSpec((1, MOD_ROWS, tn), lambda l, j: (l, 0, j)),
        out_shape=jax.ShapeDtypeStruct((depth, MOD_ROWS, n), F32),
        compiler_params=_cp("arbitrary", "arbitrary"),
        name="ada_mod",
    )(cvec, ada_w, ada_b.reshape(depth, 1, n))


def _proj_kernel(*refs, splits, with_t):
    x_ref, sh_ref, sc_ref, w_ref = refs[:4]
    pos = 4
    wt_ref = None
    if with_t:
        wt_ref = refs[pos]
        pos += 1
    outs = refs[pos:]
    xm = (x_ref[...] * (1.0 + sc_ref[0]) + sh_ref[0]).astype(BF16)
    off = 0
    for o_ref, (n, _) in zip(outs, splits):
        for c0 in range(0, n, 512):
            cw = min(512, n - c0)
            o_ref[:, c0:c0 + cw] = _dot(xm, w_ref[:, off + c0:off + c0 + cw]).astype(o_ref.dtype)
        off += n
    if with_t:
        outs[len(splits)][...] = _dot_nt(wt_ref[...], xm)


def fused_proj(lay, x, shift, scale, w, splits, wt=None):
    ntot = sum(n for n, _ in splits)
    assert w.shape == (D_MODEL, ntot)
    mod_spec = pl.BlockSpec((1, 1, D_MODEL), lambda i: (lay.mod_row(i), 0, 0))
    in_specs = [pl.BlockSpec((TM, D_MODEL), lambda i: (i, 0)), mod_spec, mod_spec,
                pl.BlockSpec((D_MODEL, ntot), lambda i: (0, 0))]
    args = [x, shift, scale, w]
    out_specs = [pl.BlockSpec((TM, n), lambda i: (i, 0)) for n, _ in splits]
    out_shape = [jax.ShapeDtypeStruct((lay.t, n), dt) for n, dt in splits]
    if wt is not None:
        r = wt.shape[0]
        in_specs.append(pl.BlockSpec((r, D_MODEL), lambda i: (0, 0)))
        args.append(wt)
        out_specs.append(pl.BlockSpec((r, TM), lambda i: (0, i)))
        out_shape.append(jax.ShapeDtypeStruct((r, lay.t), F32))
    return pl.pallas_call(
        functools.partial(_proj_kernel, splits=tuple(splits), with_t=wt is not None),
        grid=(lay.nt,),
        in_specs=in_specs, out_specs=out_specs, out_shape=out_shape,
        compiler_params=_cp("parallel"),
        name="fused_proj",
    )(*args)


def _outproj_kernel(*refs, n_mix, alpha):
    x_ref = refs[0]
    mix = refs[1:1 + n_mix]
    ws = refs[1 + n_mix:1 + 2 * n_mix]
    gate_ref, g_ref, b_ref, sh_ref, sc_ref, wr_hi, wr_lo, br_ref = refs[1 + 2 * n_mix:9 + 2 * n_mix]
    xo_ref, h_ref, lg_ref = refs[9 + 2 * n_mix:]
    y = _dot(mix[0][...], ws[0][...])
    for a, w in zip(mix[1:], ws[1:]):
        y = y + _dot(a[...], w[...])
    xn = _layer_norm(alpha * x_ref[...] + gate_ref[0] * y, g_ref[...], b_ref[...])
    xo_ref[...] = xn
    h = xn * (1.0 + sc_ref[0]) + sh_ref[0]
    h_ref[...] = h
    h_hi = h.astype(BF16)
    h_lo = (h - h_hi.astype(F32)).astype(BF16)
    lg_ref[...] = _dot(h_hi, wr_hi[...]) + (_dot(h_lo, wr_hi[...]) + _dot(h_hi, wr_lo[...])) + br_ref[...]


def outproj_ln(lay, x, mixes, ws, gate, ln_g, ln_b, shift, scale, wr_hi, wr_lo, br, alpha):
    mod_spec = pl.BlockSpec((1, 1, D_MODEL), lambda i: (lay.mod_row(i), 0, 0))
    row_spec = pl.BlockSpec((1, D_MODEL), lambda i: (0, 0))
    in_specs = [pl.BlockSpec((TM, D_MODEL), lambda i: (i, 0))]
    in_specs += [pl.BlockSpec((TM, a.shape[1]), lambda i: (i, 0)) for a in mixes]
    in_specs += [pl.BlockSpec(w.shape, lambda i: (0, 0)) for w in ws]
    in_specs += [mod_spec, row_spec, row_spec, mod_spec, mod_spec,
                 pl.BlockSpec((D_MODEL, LANE), lambda i: (0, 0)), pl.BlockSpec((D_MODEL, LANE), lambda i: (0, 0)),
                 pl.BlockSpec((1, LANE), lambda i: (0, 0))]
    return pl.pallas_call(
        functools.partial(_outproj_kernel, n_mix=len(mixes), alpha=alpha),
        grid=(lay.nt,),
        in_specs=in_specs,
        out_specs=[pl.BlockSpec((TM, D_MODEL), lambda i: (i, 0)), pl.BlockSpec((TM, D_MODEL), lambda i: (i, 0)),
                   pl.BlockSpec((TM, LANE), lambda i: (i, 0))],
        out_shape=[jax.ShapeDtypeStruct((lay.t, D_MODEL), F32), jax.ShapeDtypeStruct((lay.t, D_MODEL), F32),
                   jax.ShapeDtypeStruct((lay.t, LANE), F32)],
        compiler_params=_cp("parallel"),
        name="outproj_ln",
    )(x, *mixes, *ws, gate, ln_g, ln_b, shift, scale, wr_hi, wr_lo, br)


def _na_kernel(q_ref, kl_ref, vl_ref, kc_ref, vc_ref, bt_ref, o_ref, *, ncq, rows):
    j = pl.program_id(1)
    q = q_ref[...]
    kc = kc_ref[...]
    vc = vc_ref[...]
    lane = lax.broadcasted_iota(jnp.int32, (1, NA_DIM), 1)
    scale = NA_HEAD_DIM ** -0.5

    def heads(local):
        masks = [(lane >= h * NA_HEAD_DIM) & (lane < (h + 1) * NA_HEAD_DIM) for h in range(NA_HEADS)]
        qs = jnp.concatenate([jnp.where(hm, q, jnp.zeros_like(q)) for hm in masks], axis=0)
        s_c = _dot_nt(qs, kc) * scale
        m = jnp.max(s_c, -1, keepdims=True)
        if local:
            r = j - ncq
            r0 = jnp.clip(r - NA_ROWS // 2, 0, rows - NA_ROWS)
            dr0 = r0 - r + (NA_ROWS - 1)
            start = pl.multiple_of(r0 * GRID_W, GRID_W)
            kb = kl_ref[pl.ds(start, NA_ROWS * GRID_W), :]
            vb = vl_ref[pl.ds(start, NA_ROWS * GRID_W), :]
            s_l = _dot_nt(qs, kb) * scale + bt_ref[dr0].reshape(NA_HEADS * GRID_W, NA_ROWS * GRID_W)
            m = jnp.maximum(m, jnp.max(s_l, -1, keepdims=True))
            p_l = jnp.exp(s_l - m)
        p_c = jnp.exp(s_c - m)
        den = jnp.sum(p_c, -1, keepdims=True)
        o = _dot(p_c.astype(BF16), vc)
        if local:
            den = den + jnp.sum(p_l, -1, keepdims=True)
            o = o + _dot(p_l.astype(BF16), vb)
        o = o / den
        acc = jnp.zeros((GRID_W, NA_DIM), F32)
        for h, hm in enumerate(masks):
            acc = jnp.where(hm, o[h * GRID_W:(h + 1) * GRID_W], acc)
        o_ref[...] = acc.astype(o_ref.dtype)

    @pl.when(j < ncq)
    def _():
        heads(False)

    @pl.when(j >= ncq)
    def _():
        heads(True)


def na_attention(lay, qkv, bias_tab):
    b, c, l = lay.b, lay.c, lay.l
    ncq = c // GRID_W
    rows = lay.rows
    assert rows >= NA_ROWS
    qb = lambda bi, j: jnp.where(j < ncq, bi * ncq + j, lay.nc // GRID_W + bi * rows + (j - ncq))
    return pl.pallas_call(
        functools.partial(_na_kernel, ncq=ncq, rows=rows),
        grid=(b, ncq + rows),
        in_specs=[pl.BlockSpec((GRID_W, NA_DIM), lambda bi, j: (qb(bi, j), 0)),
                  pl.BlockSpec((l, NA_DIM), lambda bi, j: (lay.nc // l + bi, 1)),
                  pl.BlockSpec((l, NA_DIM), lambda bi, j: (lay.nc // l + bi, 2)),
                  pl.BlockSpec((c, NA_DIM), lambda bi, j: (bi, 1)),
                  pl.BlockSpec((c, NA_DIM), lambda bi, j: (bi, 2)),
                  pl.BlockSpec(bias_tab.shape, lambda bi, j: (0, 0, 0, 0))],
        out_specs=pl.BlockSpec((GRID_W, NA_DIM), lambda bi, j: (qb(bi, j), 0)),
        out_shape=jax.ShapeDtypeStruct((lay.t, NA_DIM), BF16),
        compiler_params=_cp("parallel", "arbitrary"),
        name="na_attention",
    )(qkv, qkv, qkv, qkv, qkv, bias_tab)


def na_bias_table(rpb):
    kw = NA_COLS
    col_start = np.clip(np.arange(GRID_W) - kw // 2, 0, GRID_W - kw)
    kc = np.arange(GRID_W)[None, :]
    jq = np.arange(GRID_W)[:, None]
    inside = (kc >= col_start[:, None]) & (kc < col_start[:, None] + kw)
    dc = np.clip(kc - jq + (NA_COLS - 1), 0, 2 * NA_COLS - 2)
    planes = jnp.where(inside[None, None], rpb[:, :, dc], NEG)
    tabs = []
    for dr0 in range(NA_ROWS):
        sel = planes[:, dr0:dr0 + NA_ROWS]
        tabs.append(jnp.transpose(sel, (0, 2, 1, 3)).reshape(NA_HEADS, GRID_W, NA_ROWS * GRID_W))
    return jnp.stack(tabs, 0).astype(F32)


def _conv_kernel(x_ref, p_ref, n_ref, w_ref, b_ref, o_ref, *, nct, tpc, tpl):
    i = pl.program_id(0)
    first = jnp.where(i < nct, i % tpc == 0, (i - nct) % tpl == 0)
    last = jnp.where(i < nct, i % tpc == tpc - 1, (i - nct) % tpl == tpl - 1)
    half = SSD_CONV // 2
    for c0 in range(0, XBC_DIM, 512):
        cs = slice(c0, c0 + 512)
        prev = jnp.where(first, 0.0, p_ref[:, cs])
        nxt = jnp.where(last, 0.0, n_ref[:, cs])
        ext = jnp.concatenate([prev, x_ref[:, cs], nxt], axis=0)
        n_ext = TM + 16
        acc = jnp.zeros((TM, 512), F32) + b_ref[:, cs]
        for k in range(SSD_CONV):
            shifted = ext if k == half else pltpu.roll(ext, (half - k) % n_ext, 0)
            acc = acc + w_ref[k:k + 1, cs] * shifted[8:8 + TM]
        o_ref[:, cs] = _silu(acc)


def conv_silu(lay, xbc, w8, bias):
    n8 = lay.t // 8
    return pl.pallas_call(
        functools.partial(_conv_kernel, nct=lay.nct, tpc=lay.c // TM, tpl=lay.tpb),
        grid=(lay.nt,),
        in_specs=[pl.BlockSpec((TM, XBC_DIM), lambda i: (i, 0)),
                  pl.BlockSpec((8, XBC_DIM), lambda i: (jnp.maximum(i * (TM // 8) - 1, 0), 0)),
                  pl.BlockSpec((8, XBC_DIM), lambda i: (jnp.minimum((i + 1) * (TM // 8), n8 - 1), 0)),
                  pl.BlockSpec((8, XBC_DIM), lambda i: (0, 0)),
                  pl.BlockSpec((1, XBC_DIM), lambda i: (0, 0))],
        out_specs=pl.BlockSpec((TM, XBC_DIM), lambda i: (i, 0)),
        out_shape=jax.ShapeDtypeStruct((lay.t, XBC_DIM), F32),
        compiler_params=_cp("parallel"),
        name="conv_silu",
    )(xbc, xbc, xbc, w8, bias)


def _ssd_kernel(*refs, rev):
    if rev:
        (u_ref, dt_ref, dtt_ref, bn_ref, an_ref, bc_ref, ac_ref, e_ref,
         yf_ref, z_ref, dsk_ref, gn_ref, o_ref, state, ybuf) = refs
    else:
        u_ref, dt_ref, dtt_ref, bn_ref, an_ref, bc_ref, ac_ref, e_ref, o_ref, state, ybuf = refs
    qn = SSD_CHUNK

    @pl.when(pl.program_id(1) == 0)
    def _():
        state[...] = jnp.zeros_like(state)

    xs = u_ref[:, :SSD_INNER]
    dt = _softplus(dt_ref[...] + bn_ref[...])
    a = dt * an_ref[...]
    dt_t = _softplus(dtt_ref[...] + bc_ref[...])
    a_t = dt_t * ac_ref[...]
    li = lax.broadcasted_iota(jnp.int32, (qn, qn), 0)
    si = lax.broadcasted_iota(jnp.int32, (qn, qn), 1)
    keep = (si >= li) if rev else (si <= li)
    tri = keep.astype(BF16)
    tri_t = ((li >= si) if rev else (li <= si)).astype(BF16)
    a_cs = _dot3_r(tri, a)
    a_cs_t = _dot3_l(a_t, tri_t)
    e = e_ref[...]
    dt_x = _dot3_l(dt, e)
    acs_x = _dot3_l(a_cs, e)
    end = 0 if rev else qn - 1
    acs_end = acs_x[end:end + 1, :]
    xdt = xs * dt_x
    xdt_b = xdt.astype(BF16)
    xdte = (xdt * jnp.exp(acs_end - acs_x)).astype(BF16)
    ea_x = jnp.exp(acs_x)
    cd_x = jnp.exp(acs_end)
    lane = lax.broadcasted_iota(jnp.int32, (1, LANE), 1)
    lo = lane < SSD_HEAD_DIM
    cb = None
    for p in range(SSD_HEADS // 2):
        g = p // 2
        ps = slice(p * LANE, (p + 1) * LANE)
        if p % 2 == 0:
            bg = u_ref[:, SSD_INNER + g * SSD_STATE:SSD_INNER + (g + 1) * SSD_STATE]
            cg = u_ref[:, SSD_INNER + SSD_BC + g * SSD_STATE:SSD_INNER + SSD_BC + (g + 1) * SSD_STATE].astype(BF16)
            cb = _dot_nt(cg, bg.astype(BF16))
            bg_t = bg.T.astype(BF16)
        xp = xdt_b[:, ps]
        y = None
        for hh in range(2):
            h = 2 * p + hh
            seg = a_cs[:, h:h + 1] - a_cs_t[h:h + 1, :]
            m = (jnp.exp(jnp.where(keep, seg, NEG)) * cb).astype(BF16)
            xh = jnp.where(lo if hh == 0 else jnp.logical_not(lo), xp, jnp.zeros_like(xp))
            yh = _dot(m, xh)
            y = yh if y is None else y + yh
        st = state[p]
        y = y + _dot(cg, st.astype(BF16)) * ea_x[:, ps]
        state[p] = cd_x[:, ps] * st + _dot(bg_t, xdte[:, ps])
        ybuf[:, ps] = y
    if rev:
        y = dsk_ref[...] * xs + yf_ref[...] + ybuf[...]
        y = y * _silu(z_ref[...])
        o_ref[...] = (y * lax.rsqrt(jnp.mean(y * y, -1, keepdims=True) + NORM_EPS) * gn_ref[...]).astype(o_ref.dtype)
    else:
        o_ref[...] = ybuf[...]


def ssd_pass(lay, rev, u, dt, dtt, bias_nat, a_nat, bias_col, a_col, expand, extra=None):
    qn = SSD_CHUNK
    ncc, ncl = lay.c // qn, lay.l // qn
    nctq = lay.nc // qn
    d = 1 if rev else 0

    def chunk(bi, j):
        jc = (ncc - 1 - j) if rev else j
        jl = (ncl - 1 - (j - ncc)) if rev else (j - ncc)
        return jnp.where(j < ncc, bi * ncc + jc, nctq + bi * ncl + jl)

    row = lambda n: pl.BlockSpec((1, n), lambda bi, j: (0, 0))
    in_specs = [pl.BlockSpec((qn, XBC_DIM), lambda bi, j: (chunk(bi, j), 0)),
                pl.BlockSpec((qn, LANE), lambda bi, j: (chunk(bi, j), d)),
                pl.BlockSpec((SSD_HEADS, qn), lambda bi, j: (d, chunk(bi, j))),
                row(LANE), row(LANE),
                pl.BlockSpec((SSD_HEADS, 1), lambda bi, j: (0, 0)), pl.BlockSpec((SSD_HEADS, 1), lambda bi, j: (0, 0)),
                pl.BlockSpec((LANE, SSD_INNER), lambda bi, j: (0, 0))]
    args = [u, dt, dtt, bias_nat, a_nat, bias_col, a_col, expand]
    if rev:
        yf, z, dsk, gn = extra
        in_specs += [pl.BlockSpec((qn, SSD_INNER), lambda bi, j: (chunk(bi, j), 0)),
                     pl.BlockSpec((qn, SSD_INNER), lambda bi, j: (chunk(bi, j), 0)),
                     row(SSD_INNER), row(SSD_INNER)]
        args += [yf, z, dsk, gn]
    return pl.pallas_call(
        functools.partial(_ssd_kernel, rev=rev),
        grid=(lay.b, ncc + ncl),
        in_specs=in_specs,
        out_specs=pl.BlockSpec((qn, SSD_INNER), lambda bi, j: (chunk(bi, j), 0)),
        out_shape=jax.ShapeDtypeStruct((lay.t, SSD_INNER), BF16 if rev else F32),
        scratch_shapes=[pltpu.VMEM((SSD_HEADS // 2, SSD_STATE, LANE), F32), pltpu.VMEM((qn, SSD_INNER), F32)],
        compiler_params=_cp("parallel", "arbitrary"),
        name="ssd_bwd" if rev else "ssd_fwd",
    )(*args)


def _rope_tables(lay):
    t = np.arange(lay.l)
    pos_r, pos_c = t // GRID_W, t % GRID_W

    def half_tables(half):
        inv = ROPE_THETA ** (-np.arange(0, half, 2, dtype=np.float32) / half)
        cos, sin = [], []
        for pos in (pos_r, pos_c):
            ang = pos.astype(np.float32)[:, None] * inv[None, :]
            cos += [np.cos(ang), np.cos(ang)]
            sin += [-np.sin(ang), np.sin(ang)]
        return np.concatenate(cos, 1), np.concatenate(sin, 1)

    c32, s32 = half_tables(MLA_ROPE // 2)
    c64, s64 = half_tables(GQA_HEAD_DIM // 2)
    mla_c = np.concatenate([np.ones((lay.l, MLA_NOPE)), c32, np.zeros((lay.l, LANE - MLA_NOPE - MLA_ROPE))], 1)
    mla_s = np.concatenate([np.zeros((lay.l, MLA_NOPE)), s32, np.zeros((lay.l, LANE - MLA_NOPE - MLA_ROPE))], 1)
    ident_c = np.concatenate([np.ones((TM, MLA_NOPE + MLA_ROPE)), np.zeros((TM, LANE - MLA_NOPE - MLA_ROPE))], 1)
    gqa_c = np.concatenate([c64, c64], 1)
    gqa_s = np.concatenate([s64, s64], 1)
    f = lambda ident, tab: jnp.asarray(np.concatenate([ident, tab], 0), F32)
    return (f(ident_c, mla_c), f(np.zeros((TM, LANE)), mla_s),
            f(np.ones((TM, LANE)), gqa_c), f(np.zeros((TM, LANE)), gqa_s))


def _prep_kernel(cq_ref, ckv_ref, gq_ref, gqp_ref, gk_ref, gkp_ref, kra_ref, krb_ref,
                 qn_ref, kvn_ref, gqn_ref, gqnp_ref, gkn_ref, gknp_ref,
                 wq_ref, wqp_ref, wk_ref, wv_ref, bd_ref, mc_ref, ms_ref, gc_ref, gs_ref,
                 mq_ref, mk_ref, mv_ref, oq_ref, ok_ref):
    def rms(x, g):
        return x * lax.rsqrt(jnp.mean(x * x, -1, keepdims=True) + NORM_EPS) * g

    mc, ms = mc_ref[...], ms_ref[...]
    mc8 = jnp.concatenate([mc] * MLA_HEADS, 1)
    ms8 = jnp.concatenate([ms] * MLA_HEADS, 1)
    rq = rms(cq_ref[...], qn_ref[...]).astype(BF16)
    q = _dot(rq, wq_ref[...]) * mc8 + _dot(rq, wqp_ref[...]) * ms8
    mq_ref[...] = (q * MLA_SCALE).astype(BF16)
    rkv = rms(ckv_ref[...], kvn_ref[...]).astype(BF16)
    kr = kra_ref[...] * mc + krb_ref[...] * ms
    mk_ref[...] = (_dot(rkv, wk_ref[...]) + jnp.concatenate([kr] * MLA_HEADS, 1)).astype(BF16)
    mv_ref[...] = _dot(rkv, wv_ref[...]).astype(BF16)
    gc4 = jnp.concatenate([gc_ref[...]] * 4, 1)
    gs4 = jnp.concatenate([gs_ref[...]] * 4, 1)
    bd = bd_ref[...]

    def qk_norm_rope(u, up, g, gp, scale):
        sq = u * u
        hi = sq.astype(BF16)
        lo = (sq - hi.astype(F32)).astype(BF16)
        r = lax.rsqrt(_dot(hi, bd) + _dot(lo, bd) + NORM_EPS)
        return ((u * g * gc4 + up * gp * gs4) * (r * scale)).astype(BF16)

    oq_ref[...] = qk_norm_rope(gq_ref[...], gqp_ref[...], gqn_ref[...], gqnp_ref[...], GQA_SCALE)
    ok_ref[...] = qk_norm_rope(gk_ref[...], gkp_ref[...], gkn_ref[...], gknp_ref[...], 1.0)


def odd_prep(lay, acts, norms, weights, tables):
    tab_row = lambda i: jnp.where(i < lay.nct, 0, 1 + (i - lay.nct) % lay.tpb)
    tok = lambda n: pl.BlockSpec((TM, n), lambda i: (i, 0))
    const = lambda a: pl.BlockSpec(a.shape, lambda i: (0, 0))
    in_specs = [tok(a.shape[1]) for a in acts] + [const(a) for a in norms] + [const(a) for a in weights]
    in_specs += [pl.BlockSpec((TM, LANE), lambda i: (tab_row(i), 0)) for _ in tables]
    widths = (D_MODEL, D_MODEL, D_MODEL, 512, 512)
    return pl.pallas_call(
        _prep_kernel,
        grid=(lay.nt,),
        in_specs=in_specs,
        out_specs=[tok(n) for n in widths],
        out_shape=[jax.ShapeDtypeStruct((lay.t, n), BF16) for n in widths],
        compiler_params=_cp("parallel"),
        name="odd_prep",
    )(*acts, *norms, *weights, *tables)


def _attn_kernel(q_ref, kc_ref, kl_ref, vc_ref, vl_ref, o_ref, *, ncq, q_shared):
    j = pl.program_id(2)

    def run(with_lat):
        out = None
        for hh in range(2):
            q = q_ref[...] if q_shared else q_ref[:, hh * LANE:(hh + 1) * LANE]
            hs = slice(hh * LANE, (hh + 1) * LANE)
            s_c = _dot_nt(q, kc_ref[:, hs])
            m = jnp.max(s_c, -1, keepdims=True)
            if with_lat:
                s_l = _dot_nt(q, kl_ref[:, hs])
                m = jnp.maximum(m, jnp.max(s_l, -1, keepdims=True))
                p_l = jnp.exp(s_l - m)
            p_c = jnp.exp(s_c - m)
            den = jnp.sum(p_c, -1, keepdims=True)
            o = _dot(p_c.astype(BF16), vc_ref[:, hs])
            if with_lat:
                den = den + jnp.sum(p_l, -1, keepdims=True)
                o = o + _dot(p_l.astype(BF16), vl_ref[:, hs])
            o = o / den
            out = o if out is None else out + o
        o_ref[...] = out.astype(o_ref.dtype)

    @pl.when(j < ncq)
    def _():
        run(False)

    @pl.when(j >= ncq)
    def _():
        run(True)


def dense_attention(lay, q, k, v, q_shared, kv_block):
    b, c, l = lay.b, lay.c, lay.l
    tq = ATT_TQ
    ncq, nlq = c // tq, l // tq
    qw = LANE if q_shared else 2 * LANE
    qb = lambda bi, j: jnp.where(j < ncq, bi * ncq + j, lay.nc // tq + bi * nlq + (j - ncq))
    ctx_spec = pl.BlockSpec((c, 2 * LANE), lambda bi, p, j: (bi, kv_block(p)))
    lat_spec = pl.BlockSpec((l, 2 * LANE), lambda bi, p, j: (lay.nc // l + bi, kv_block(p)))
    return pl.pallas_call(
        functools.partial(_attn_kernel, ncq=ncq, q_shared=q_shared),
        grid=(b, 4, ncq + nlq),
        in_specs=[pl.BlockSpec((tq, qw), lambda bi, p, j: (qb(bi, j), p)), ctx_spec, lat_spec, ctx_spec, lat_spec],
        out_specs=pl.BlockSpec((tq, LANE), lambda bi, p, j: (qb(bi, j), p)),
        out_shape=jax.ShapeDtypeStruct((lay.t, 4 * LANE), BF16),
        compiler_params=_cp("parallel", "parallel", "arbitrary"),
        name="dense_attention",
    )(q, k, k, v, v)


GATHER_UNROLL = 8


def _moe_ffn_kernel(be_ref, nb_ref, tok_hbm, h_hbm, sw_ref, w1_ref, w3_ref, w2_ref, y_ref, idx, xbuf, sem_i, sem_g):
    i = pl.program_id(0)
    n_used = nb_ref[0]

    def idx_copy(blk):
        slot = blk % 2
        return pltpu.make_async_copy(tok_hbm.at[pl.ds(blk * MOE_TM, MOE_TM)], idx.at[slot], sem_i.at[slot])

    def gather(blk):
        slot = blk % 2

        def issue(r, carry):
            pltpu.make_async_copy(h_hbm.at[pl.ds(idx[slot, r], 1), :], xbuf.at[slot, pl.ds(r, 1), :],
                                  sem_g.at[slot]).start()
            return carry

        lax.fori_loop(0, MOE_TM, issue, 0, unroll=GATHER_UNROLL)

    @pl.when((i == 0) & (n_used > 0))
    def _():
        idx_copy(0).start()
        idx_copy(0).wait()
        gather(0)

        @pl.when(n_used > 1)
        def _():
            idx_copy(1).start()

    @pl.when(i + 1 < n_used)
    def _():
        idx_copy(i + 1).wait()
        gather(i + 1)

    @pl.when(i + 2 < n_used)
    def _():
        idx_copy(i + 2).start()

    @pl.when(i < n_used)
    def _():
        slot = i % 2
        pltpu.make_async_copy(h_hbm.at[pl.ds(0, MOE_TM), :], xbuf.at[slot], sem_g.at[slot]).wait()
        x = xbuf[slot].astype(BF16)
        hid = (_silu(_dot(x, w1_ref[0])) * _dot(x, w3_ref[0])).astype(BF16)
        y_ref[...] = _dot(hid, w2_ref[0]) * sw_ref[...]

    @pl.when(i >= n_used)
    def _():
        y_ref[...] = jnp.zeros_like(y_ref)


def moe_ffn(h, blk_e, n_used, slot_tok, slot_w, w1, w3, w2):
    nb = blk_e.shape[0]
    wspec = lambda shape: pl.BlockSpec((1,) + shape, lambda i, be, nu: (be[i], 0, 0))
    return pl.pallas_call(
        _moe_ffn_kernel,
        grid_spec=pltpu.PrefetchScalarGridSpec(
            num_scalar_prefetch=2,
            grid=(nb,),
            in_specs=[pl.BlockSpec(memory_space=pl.ANY), pl.BlockSpec(memory_space=pl.ANY),
                      pl.BlockSpec((MOE_TM, 1), lambda i, be, nu: (i, 0)),
                      wspec((D_MODEL, MOE_HIDDEN)), wspec((D_MODEL, MOE_HIDDEN)), wspec((MOE_HIDDEN, D_MODEL))],
            out_specs=pl.BlockSpec((MOE_TM, D_MODEL), lambda i, be, nu: (i, 0)),
            scratch_shapes=[pltpu.SMEM((2, MOE_TM), jnp.int32), pltpu.VMEM((2, MOE_TM, D_MODEL), F32),
                            pltpu.SemaphoreType.DMA((2,)), pltpu.SemaphoreType.DMA((2,))]),
        out_shape=jax.ShapeDtypeStruct((nb * MOE_TM, D_MODEL), F32),
        compiler_params=_cp("arbitrary"),
        name="moe_ffn",
    )(blk_e, n_used, slot_tok, h, slot_w, w1, w3, w2)


def _combine_kernel(pos_hbm, y_hbm, x_ref, gate_ref, g_ref, b_ref, o_ref, idx, buf, sem_i, sem_g, *, alpha):
    i = pl.program_id(0)
    n = pl.num_programs(0)

    def idx_copy(tile):
        slot = tile % 2
        return pltpu.make_async_copy(pos_hbm.at[pl.ds(tile * (MOE_TOPK * TM), MOE_TOPK * TM)], idx.at[slot],
                                     sem_i.at[slot])

    def gather(tile):
        slot = tile % 2

        def issue(r, carry):
            for k in range(MOE_TOPK):
                pltpu.make_async_copy(y_hbm.at[pl.ds(idx[slot, MOE_TOPK * r + k], 1), :],
                                      buf.at[slot, k, pl.ds(r, 1), :], sem_g.at[slot]).start()
            return carry

        lax.fori_loop(0, TM, issue, 0, unroll=GATHER_UNROLL // MOE_TOPK)

    @pl.when(i == 0)
    def _():
        idx_copy(0).start()
        idx_copy(0).wait()
        gather(0)

        @pl.when(n > 1)
        def _():
            idx_copy(1).start()

    @pl.when(i + 1 < n)
    def _():
        idx_copy(i + 1).wait()
        gather(i + 1)

    @pl.when(i + 2 < n)
    def _():
        idx_copy(i + 2).start()

    slot = i % 2
    for k in range(MOE_TOPK):
        pltpu.make_async_copy(y_hbm.at[pl.ds(0, TM), :], buf.at[slot, k], sem_g.at[slot]).wait()
    f = buf[slot, 0] + buf[slot, 1]
    o_ref[...] = _layer_norm(alpha * x_ref[...] + gate_ref[0] * f, g_ref[...], b_ref[...])


def moe_combine_ln(lay, pos, y, x, gate, ln_g, ln_b, alpha):
    row_spec = pl.BlockSpec((1, D_MODEL), lambda i: (0, 0))
    return pl.pallas_call(
        functools.partial(_combine_kernel, alpha=alpha),
        grid=(lay.nt,),
        in_specs=[pl.BlockSpec(memory_space=pl.ANY), pl.BlockSpec(memory_space=pl.ANY),
                  pl.BlockSpec((TM, D_MODEL), lambda i: (i, 0)),
                  pl.BlockSpec((1, 1, D_MODEL), lambda i: (lay.mod_row(i), 0, 0)), row_spec, row_spec],
        out_specs=pl.BlockSpec((TM, D_MODEL), lambda i: (i, 0)),
        out_shape=jax.ShapeDtypeStruct((lay.t, D_MODEL), F32),
        scratch_shapes=[pltpu.SMEM((2, MOE_TOPK * TM), jnp.int32), pltpu.VMEM((2, MOE_TOPK, TM, D_MODEL), F32),
                        pltpu.SemaphoreType.DMA((2,)), pltpu.SemaphoreType.DMA((2,))],
        compiler_params=_cp("arbitrary"),
        name="moe_combine_ln",
    )(pos, y, x, gate, ln_g, ln_b)


def moe_route(logits):
    t = logits.shape[0]
    g_logit = logits[:, :MOE_GROUPS]
    g_sel = jnp.argmax(g_logit, axis=-1).astype(jnp.int32)
    g_gate = 1.0 / jnp.sum(jnp.exp(g_logit - jnp.max(g_logit, -1, keepdims=True)), -1)
    e_logits = logits[:, MOE_GROUPS:MOE_GROUPS + MOE_EXPERTS].reshape(t, MOE_GROUPS, MOE_EPG)
    g_hot = g_sel[:, None] == jnp.arange(MOE_GROUPS, dtype=jnp.int32)[None, :]
    e_in = jnp.sum(jnp.where(g_hot[:, :, None], e_logits, 0.0), axis=1)
    lanes = jnp.arange(MOE_EPG, dtype=jnp.int32)[None, :]
    i1 = jnp.argmax(e_in, axis=-1).astype(jnp.int32)
    v1 = jnp.max(e_in, axis=-1)
    rest = jnp.where(lanes == i1[:, None], -jnp.inf, e_in)
    i2 = jnp.argmax(rest, axis=-1).astype(jnp.int32)
    v2 = jnp.max(rest, axis=-1)
    p2 = jnp.exp(v2 - v1)
    gate = g_gate[:, None] * jnp.stack([1.0 / (1.0 + p2), p2 / (1.0 + p2)], -1)
    expert = g_sel[:, None] * MOE_EPG + jnp.stack([i1, i2], -1)
    return expert.astype(jnp.int32), gate


def moe_dispatch_plan(expert, gate):
    t = expert.shape[0]
    n_assign = t * MOE_TOPK
    flat_e = expert.reshape(n_assign)
    flat_w = gate.reshape(n_assign)
    eids = jnp.arange(MOE_EXPERTS, dtype=jnp.int32)
    order = jnp.argsort(flat_e).astype(jnp.int32)
    rank = jnp.argsort(order).astype(jnp.int32)
    hot = flat_e[:, None] == eids[None, :]
    counts = jnp.sum(hot, axis=0, dtype=jnp.int32)
    starts = jnp.cumsum(counts) - counts
    pcounts = (counts + MOE_TM - 1) // MOE_TM * MOE_TM
    pends = jnp.cumsum(pcounts)
    pstarts = pends - pcounts
    nb = (n_assign + MOE_EXPERTS * (MOE_TM - 1)) // MOE_TM
    blk_start = jnp.arange(nb, dtype=jnp.int32) * MOE_TM
    blk_e = jnp.minimum(jnp.sum(pends[None, :] <= blk_start[:, None], axis=1), MOE_EXPERTS - 1).astype(jnp.int32)
    n_used = (pends[-1] // MOE_TM).astype(jnp.int32).reshape(1)
    blk_hot = blk_e[:, None] == eids[None, :]
    pick = lambda table: jnp.sum(jnp.where(blk_hot, table[None, :], 0), axis=1)[:, None]
    in_blk = jnp.arange(MOE_TM, dtype=jnp.int32)[None, :]
    seg_rank = blk_start[:, None] + in_blk - pick(pstarts)
    valid = (seg_rank < pick(counts)).reshape(-1)
    src = jnp.clip(pick(starts) + seg_rank, 0, n_assign - 1)
    assign = order[src.reshape(-1)]
    slot_tok = jnp.where(valid, assign // MOE_TOPK, 0).astype(jnp.int32)
    slot_w = jnp.where(valid, flat_w[assign], 0.0).astype(F32)
    pos = (rank + jnp.sum(jnp.where(hot, (pstarts - starts)[None, :], 0), axis=1)).astype(jnp.int32)
    return blk_e, n_used, slot_tok, slot_w.reshape(-1, 1), pos


def _rope_partner(dim):
    q = dim // 4
    idx = np.arange(dim)
    return np.where((idx // q) % 2 == 0, idx + q, idx - q)


def _even_weights(w_in, w_out, conv_w, conv_b, dt_bias, a_log, d_skip, gnorm_w):
    n_main = NA_DIM * 3 + SSD_INNER + XBC_DIM
    w_dt = jnp.zeros((D_MODEL, 2 * LANE), F32)
    w_dt = w_dt.at[:, :SSD_HEADS].set(w_in[:, n_main:n_main + SSD_HEADS])
    w_dt = w_dt.at[:, LANE:LANE + SSD_HEADS].set(w_in[:, n_main + SSD_HEADS:])
    w = jnp.concatenate([w_in[:, :n_main], w_dt], 1).astype(BF16)
    wt = w_in[:, n_main:].T.astype(BF16)
    pad = lambda v: jnp.zeros((1, LANE), F32).at[0, :SSD_HEADS].set(v)
    a = -jnp.exp(a_log.astype(F32))
    expand = np.zeros((LANE, SSD_INNER), np.float32)
    for h in range(SSD_HEADS):
        expand[h, h * SSD_HEAD_DIM:(h + 1) * SSD_HEAD_DIM] = 1.0
    return dict(
        w=w, wt=wt,
        conv_w=jnp.zeros((8, XBC_DIM), F32).at[:SSD_CONV].set(conv_w), conv_b=conv_b.reshape(1, XBC_DIM),
        bias_nat=[pad(dt_bias[d]) for d in range(2)], a_nat=[pad(a[d]) for d in range(2)],
        bias_col=[dt_bias[d].reshape(SSD_HEADS, 1) for d in range(2)], a_col=[a[d].reshape(SSD_HEADS, 1) for d in range(2)],
        expand=jnp.asarray(expand, BF16),
        dskip=jnp.repeat(d_skip, SSD_HEAD_DIM).reshape(1, SSD_INNER), gnorm=gnorm_w.reshape(1, SSD_INNER),
        w_out=[w_out[:NA_DIM].astype(BF16), w_out[NA_DIM:].astype(BF16)])


def _odd_weights(w_in, w_out, q_norm, kv_norm, w_uq, w_ukv, gq_norm, gk_norm):
    o = np.cumsum([0, MLA_Q_LORA, GQA_HEADS * GQA_HEAD_DIM, MLA_KV_LORA, MLA_ROPE, 128, 128])
    w_cq, w_gq, w_ckv, w_kr, w_gk, w_gv = [w_in[:, o[i]:o[i + 1]] for i in range(6)]
    p64, p32 = _rope_partner(GQA_HEAD_DIM), _rope_partner(MLA_ROPE)
    head_perm = lambda n: np.concatenate([h * GQA_HEAD_DIM + p64 for h in range(n)])
    z64 = jnp.zeros((D_MODEL, GQA_HEAD_DIM), F32)

    def placed(wm):
        out = []
        for g in range(GQA_KV_HEADS):
            wg = wm[:, g * GQA_HEAD_DIM:(g + 1) * GQA_HEAD_DIM]
            out += [wg, z64, z64, wg]
        return jnp.concatenate(out, 1)

    kr_slab = lambda wm: jnp.zeros((D_MODEL, LANE), F32).at[:, MLA_NOPE:MLA_NOPE + MLA_ROPE].set(wm)
    w = jnp.concatenate([w_cq, w_ckv, w_gq, w_gq[:, head_perm(GQA_HEADS)],
                         placed(w_gk), placed(w_gk[:, head_perm(GQA_KV_HEADS)]),
                         kr_slab(w_kr), kr_slab(w_kr[:, p32]), placed(w_gv)], 1).astype(BF16)
    splits = [(MLA_Q_LORA, F32), (MLA_KV_LORA, F32), (512, F32), (512, F32), (512, F32), (512, F32),
              (LANE, F32), (LANE, F32), (512, BF16)]
    uq = w_uq.reshape(MLA_Q_LORA, MLA_HEADS, MLA_NOPE + MLA_ROPE)
    pad_q = jnp.zeros((MLA_Q_LORA, MLA_HEADS, LANE - MLA_NOPE - MLA_ROPE), F32)
    wq = jnp.concatenate([uq, pad_q], 2).reshape(MLA_Q_LORA, D_MODEL)
    uq_par = jnp.concatenate([jnp.zeros_like(uq[:, :, :MLA_NOPE]), uq[:, :, MLA_NOPE:][:, :, p32]], 2)
    wqp = jnp.concatenate([uq_par, pad_q], 2).reshape(MLA_Q_LORA, D_MODEL)
    ukv = w_ukv.reshape(MLA_KV_LORA, MLA_HEADS, MLA_NOPE + MLA_V)
    zk = jnp.zeros((MLA_KV_LORA, MLA_HEADS, MLA_NOPE), F32)
    wk = jnp.concatenate([ukv[:, :, :MLA_NOPE], zk], 2).reshape(MLA_KV_LORA, D_MODEL)
    uv = ukv[:, :, MLA_NOPE:]
    even = (np.arange(MLA_HEADS) % 2 == 0)[None, :, None]
    wv = jnp.concatenate([jnp.where(even, uv, 0.0), jnp.where(even, 0.0, uv)], 2).reshape(MLA_KV_LORA, D_MODEL)
    bd = np.kron(np.eye(8, dtype=np.float32), np.full((64, 64), 1.0 / 64, np.float32))
    tile = lambda v, n: jnp.tile(v, n).reshape(1, -1)
    return dict(
        w=w, splits=splits,
        norms=[q_norm.reshape(1, -1), kv_norm.reshape(1, -1), tile(gq_norm, 8), tile(gq_norm[p64], 8),
               tile(gk_norm, 8), tile(gk_norm[p64], 8)],
        weights=[wq.astype(BF16), wqp.astype(BF16), wk.astype(BF16), wv.astype(BF16), jnp.asarray(bd, BF16)],
        w_out=[w_out[:512].astype(BF16), w_out[512:].astype(BF16)])


def kernel(x, c, ctx, c_ctx, ada_w, ada_b, ln_g, ln_b, e_w_in, e_w_out, e_conv_w, e_conv_b, e_dt_bias, e_a_log, e_d_skip, e_gnorm_w, e_rpb, o_w_in, o_w_out, o_mla_q_norm, o_mla_kv_norm, o_w_uq, o_w_ukv, o_gqa_q_norm, o_gqa_k_norm, moe_wg, moe_bg, moe_we, moe_be, moe_w1, moe_w3, moe_w2):
    b, l, dm = x.shape
    cl = ctx.shape[1]
    depth = ada_w.shape[0]
    lay = _Layout(b, cl, l)
    alpha = (2.0 * depth) ** 0.25
    xa = jnp.concatenate([ctx.reshape(b * cl, dm), x.reshape(b * l, dm)], 0)
    cvec = jnp.zeros((MOD_ROWS, dm), F32).at[:b].set(c).at[b].set(c_ctx)
    mods = ada_mod(cvec, ada_w, ada_b).reshape(depth, MOD_ROWS, N_MOD, 1, dm).transpose(0, 2, 1, 3, 4)
    rope_tabs = _rope_tables(lay)
    row = lambda v: v.reshape(1, -1)
    for layer in range(depth):
        i = layer // 2
        mod = [mods[layer, k] for k in range(N_MOD)]
        if layer % 2 == 0:
            ew = _even_weights(e_w_in[i], e_w_out[i], e_conv_w[i], e_conv_b[i], e_dt_bias[i], e_a_log[i],
                               e_d_skip[i], e_gnorm_w[i])
            qkv, z, xbc, dt, dtt = fused_proj(
                lay, xa, mod[0], mod[1], ew['w'],
                [(3 * NA_DIM, BF16), (SSD_INNER, F32), (XBC_DIM, F32), (2 * LANE, F32)], wt=ew['wt'])
            na = na_attention(lay, qkv, na_bias_table(e_rpb[i]))
            u = conv_silu(lay, xbc, ew['conv_w'], ew['conv_b'])
            yf = ssd_pass(lay, False, u, dt, dtt, ew['bias_nat'][0], ew['a_nat'][0], ew['bias_col'][0],
                          ew['a_col'][0], ew['expand'])
            ssd = ssd_pass(lay, True, u, dt, dtt, ew['bias_nat'][1], ew['a_nat'][1], ew['bias_col'][1],
                           ew['a_col'][1], ew['expand'], extra=(yf, z, ew['dskip'], ew['gnorm']))
            mixes, w_outs = [na, ssd], ew['w_out']
        else:
            ow = _odd_weights(o_w_in[i], o_w_out[i], o_mla_q_norm[i], o_mla_kv_norm[i], o_w_uq[i], o_w_ukv[i],
                              o_gqa_q_norm[i], o_gqa_k_norm[i])
            cq, ckv, gq, gqp, gk, gkp, kra, krb, gv = fused_proj(lay, xa, mod[0], mod[1], ow['w'], ow['splits'])
            mq, mk, mv, gqr, gkr = odd_prep(lay, [cq, ckv, gq, gqp, gk, gkp, kra, krb], ow['norms'], ow['weights'],
                                            rope_tabs)
            o_mla = dense_attention(lay, mq, mk, mv, False, lambda p: p)
            o_gqa = dense_attention(lay, gqr, gkr, gv, True, lambda p: p // 2)
            mixes, w_outs = [o_mla, o_gqa], ow['w_out']
        wr = jnp.zeros((dm, LANE), F32).at[:, :MOE_GROUPS].set(moe_wg[layer])
        wr = wr.at[:, MOE_GROUPS:MOE_GROUPS + MOE_EXPERTS].set(moe_we[layer])
        wr_hi = wr.astype(BF16)
        wr_lo = (wr - wr_hi.astype(F32)).astype(BF16)
        br = jnp.zeros((1, LANE), F32).at[0, :MOE_GROUPS].set(moe_bg[layer])
        br = br.at[0, MOE_GROUPS:MOE_GROUPS + MOE_EXPERTS].set(moe_be[layer])
        xa, h2, logits = outproj_ln(lay, xa, mixes, w_outs, mod[2], row(ln_g[layer, 0]), row(ln_b[layer, 0]),
                                    mod[3], mod[4], wr_hi, wr_lo, br, alpha)
        expert, gate = moe_route(logits)
        blk_e, n_used, slot_tok, slot_w, pos = moe_dispatch_plan(expert, gate)
        y = moe_ffn(h2, blk_e, n_used, slot_tok, slot_w, moe_w1[layer].astype(BF16), moe_w3[layer].astype(BF16),
                    moe_w2[layer].astype(BF16))
        xa = moe_combine_ln(lay, pos, y, xa, mod[5], row(ln_g[layer, 1]), row(ln_b[layer, 1]), alpha)
    return xa[lay.nc:].reshape(b, l, dm)
```

```python
import functools
import math

import numpy as np
import jax
import jax.numpy as jnp
from jax import lax
from jax.experimental import pallas as pl
from jax.experimental.pallas import tpu as pltpu

F32 = jnp.float32
BF16 = jnp.bfloat16

D_MODEL = 1024
GRID_W = 64
NORM_EPS = 1e-6
ROPE_THETA = 10000.0
N_MOD = 6

SSD_HEADS = 16
SSD_HEAD_DIM = 64
SSD_INNER = 1024
SSD_GROUPS = 4
SSD_STATE = 128
SSD_BC = 512
XBC_DIM = 2048
SSD_CONV = 5
SSD_CHUNK = 128

NA_HEADS = 8
NA_HEAD_DIM = 64
NA_DIM = 512
NA_ROWS = 8
NA_COLS = 16

MLA_HEADS = 8
MLA_Q_LORA = 384
MLA_KV_LORA = 256
MLA_NOPE = 64
MLA_ROPE = 32
MLA_V = 64
MLA_SCALE = (MLA_NOPE + MLA_ROPE) ** -0.5
GQA_HEADS = 8
GQA_KV_HEADS = 2
GQA_HEAD_DIM = 64
GQA_SCALE = GQA_HEAD_DIM ** -0.5

MOE_GROUPS = 4
MOE_EPG = 8
MOE_EXPERTS = 32
MOE_TOPK = 2
MOE_HIDDEN = 512

LANE = 128
TM = 256
MOE_TM = 256
ATT_TQ = 256
V_ONE_LANE = 64
MOD_ROWS = 24
NEG = -1e30
VMEM_LIMIT = 56 * 1024 * 1024


def _cp(*sem):
    return pltpu.CompilerParams(dimension_semantics=sem, vmem_limit_bytes=VMEM_LIMIT)


def _dot(a, b):
    return jnp.dot(a, b, preferred_element_type=F32)


def _dot_nt(a, b):
    return lax.dot_general(a, b, (((1,), (1,)), ((), ())), preferred_element_type=F32)


def _split3(x):
    x1 = x.astype(BF16)
    r = x - x1.astype(F32)
    x2 = r.astype(BF16)
    r = r - x2.astype(F32)
    return x1, x2, r.astype(BF16)


def _dot3_l(x, m):
    a, b, c = _split3(x)
    return _dot(a, m) + _dot(b, m) + _dot(c, m)


def _dot3_r(m, x):
    a, b, c = _split3(x)
    return _dot(m, a) + _dot(m, b) + _dot(m, c)


def _silu(x):
    return x * jax.nn.sigmoid(x)


def _softplus(x):
    return jnp.maximum(x, 0.0) + jnp.log1p(jnp.exp(-jnp.abs(x)))


def _layer_norm(v, g, b):
    mu = jnp.mean(v, -1, keepdims=True)
    vc = v - mu
    var = jnp.mean(vc * vc, -1, keepdims=True)
    return vc * lax.rsqrt(var + NORM_EPS) * g + b


class _Layout:
    def __init__(self, b, c, l):
        assert c % TM == 0 and l % TM == 0 and (b * c) % l == 0 and l % GRID_W == 0
        assert b + 1 <= MOD_ROWS
        self.b, self.c, self.l = b, c, l
        self.nc = b * c
        self.t = b * c + b * l
        self.nct = self.nc // TM
        self.nt = self.t // TM
        self.tpb = l // TM
        self.rows = l // GRID_W

    def mod_row(self, i):
        return jnp.where(i < self.nct, self.b, (i - self.nct) // self.tpb)


def _ada_kernel(c_ref, w_ref, b_ref, o_ref):
    s = _silu(c_ref[...]).astype(BF16)
    o_ref[0] = _dot(s, w_ref[0].astype(BF16)) + b_ref[0]


def ada_mod(cvec, ada_w, ada_b):
    depth, dm, n = ada_w.shape
    tn = 1536
    assert n % tn == 0
    return pl.pallas_call(
        _ada_kernel,
        grid=(depth, n // tn),
        in_specs=[pl.BlockSpec((MOD_ROWS, dm), lambda l, j: (0, 0)),
                  pl.BlockSpec((1, dm, tn), lambda l, j: (l, 0, j)),
                  pl.BlockSpec((1, 1, tn), lambda l, j: (l, 0, j))],
        out_specs=pl.BlockSpec((1, MOD_ROWS, tn), lambda l, j: (l, 0, j)),
        out_shape=jax.ShapeDtypeStruct((depth, MOD_ROWS, n), F32),
        compiler_params=_cp("arbitrary", "arbitrary"),
        name="ada_mod",
    )(cvec, ada_w, ada_b.reshape(depth, 1, n))


def _proj_kernel(*refs, splits, with_t):
    x_ref, sh_ref, sc_ref, w_ref = refs[:4]
    pos = 4
    wt_ref = None
    if with_t:
        wt_ref = refs[pos]
        pos += 1
    outs = refs[pos:]
    xm = (x_ref[...] * (1.0 + sc_ref[0]) + sh_ref[0]).astype(BF16)
    off = 0
    for o_ref, (n, _) in zip(outs, splits):
        for c0 in range(0, n, 512):
            cw = min(512, n - c0)
            o_ref[:, c0:c0 + cw] = _dot(xm, w_ref[:, off + c0:off + c0 + cw]).astype(o_ref.dtype)
        off += n
    if with_t:
        outs[len(splits)][...] = _dot_nt(wt_ref[...], xm)


def fused_proj(lay, x, shift, scale, w, splits, wt=None):
    ntot = sum(n for n, _ in splits)
    assert w.shape == (D_MODEL, ntot)
    mod_spec = pl.BlockSpec((1, 1, D_MODEL), lambda i: (lay.mod_row(i), 0, 0))
    in_specs = [pl.BlockSpec((TM, D_MODEL), lambda i: (i, 0)), mod_spec, mod_spec,
                pl.BlockSpec((D_MODEL, ntot), lambda i: (0, 0))]
    args = [x, shift, scale, w]
    out_specs = [pl.BlockSpec((TM, n), lambda i: (i, 0)) for n, _ in splits]
    out_shape = [jax.ShapeDtypeStruct((lay.t, n), dt) for n, dt in splits]
    if wt is not None:
        r = wt.shape[0]
        in_specs.append(pl.BlockSpec((r, D_MODEL), lambda i: (0, 0)))
        args.append(wt)
        out_specs.append(pl.BlockSpec((r, TM), lambda i: (0, i)))
        out_shape.append(jax.ShapeDtypeStruct((r, lay.t), F32))
    return pl.pallas_call(
        functools.partial(_proj_kernel, splits=tuple(splits), with_t=wt is not None),
        grid=(lay.nt,),
        in_specs=in_specs, out_specs=out_specs, out_shape=out_shape,
        compiler_params=_cp("parallel"),
        name="fused_proj",
    )(*args)


def _outproj_kernel(*refs, n_mix, alpha):
    x_ref = refs[0]
    mix = refs[1:1 + n_mix]
    ws = refs[1 + n_mix:1 + 2 * n_mix]
    gate_ref, g_ref, b_ref, sh_ref, sc_ref, wr_hi, wr_lo, br_ref = refs[1 + 2 * n_mix:9 + 2 * n_mix]
    xo_ref, h_ref, lg_ref = refs[9 + 2 * n_mix:]
    y = _dot(mix[0][...], ws[0][...])
    for a, w in zip(mix[1:], ws[1:]):
        y = y + _dot(a[...], w[...])
    xn = _layer_norm(alpha * x_ref[...] + gate_ref[0] * y, g_ref[...], b_ref[...])
    xo_ref[...] = xn
    h = xn * (1.0 + sc_ref[0]) + sh_ref[0]
    h_ref[...] = h
    h_hi = h.astype(BF16)
    h_lo = (h - h_hi.astype(F32)).astype(BF16)
    lg_ref[...] = _dot(h_hi, wr_hi[...]) + (_dot(h_lo, wr_hi[...]) + _dot(h_hi, wr_lo[...])) + br_ref[...]


def outproj_ln(lay, x, mixes, ws, gate, ln_g, ln_b, shift, scale, wr_hi, wr_lo, br, alpha):
    mod_spec = pl.BlockSpec((1, 1, D_MODEL), lambda i: (lay.mod_row(i), 0, 0))
    row_spec = pl.BlockSpec((1, D_MODEL), lambda i: (0, 0))
    in_specs = [pl.BlockSpec((TM, D_MODEL), lambda i: (i, 0))]
    in_specs += [pl.BlockSpec((TM, a.shape[1]), lambda i: (i, 0)) for a in mixes]
    in_specs += [pl.BlockSpec(w.shape, lambda i: (0, 0)) for w in ws]
    in_specs += [mod_spec, row_spec, row_spec, mod_spec, mod_spec,
                 pl.BlockSpec((D_MODEL, LANE), lambda i: (0, 0)), pl.BlockSpec((D_MODEL, LANE), lambda i: (0, 0)),
                 pl.BlockSpec((1, LANE), lambda i: (0, 0))]
    return pl.pallas_call(
        functools.partial(_outproj_kernel, n_mix=len(mixes), alpha=alpha),
        grid=(lay.nt,),
        in_specs=in_specs,
        out_specs=[pl.BlockSpec((TM, D_MODEL), lambda i: (i, 0)), pl.BlockSpec((TM, D_MODEL), lambda i: (i, 0)),
                   pl.BlockSpec((TM, LANE), lambda i: (i, 0))],
        out_shape=[jax.ShapeDtypeStruct((lay.t, D_MODEL), F32), jax.ShapeDtypeStruct((lay.t, D_MODEL), F32),
                   jax.ShapeDtypeStruct((lay.t, LANE), F32)],
        compiler_params=_cp("parallel"),
        name="outproj_ln",
    )(x, *mixes, *ws, gate, ln_g, ln_b, shift, scale, wr_hi, wr_lo, br)


def _na_kernel(q_ref, kl_ref, vl_ref, kc_ref, vc_ref, bt_ref, o_ref, *, ncq, rows):
    j = pl.program_id(1)
    q = q_ref[...]
    kc = kc_ref[...]
    vc = vc_ref[...]
    lane = lax.broadcasted_iota(jnp.int32, (1, NA_DIM), 1)
    scale = NA_HEAD_DIM ** -0.5

    def heads(local):
        masks = [(lane >= h * NA_HEAD_DIM) & (lane < (h + 1) * NA_HEAD_DIM) for h in range(NA_HEADS)]
        qs = jnp.concatenate([jnp.where(hm, q, jnp.zeros_like(q)) for hm in masks], axis=0)
        s_c = _dot_nt(qs, kc) * scale
        m = jnp.max(s_c, -1, keepdims=True)
        if local:
            r = j - ncq
            r0 = jnp.clip(r - NA_ROWS // 2, 0, rows - NA_ROWS)
            dr0 = r0 - r + (NA_ROWS - 1)
            start = pl.multiple_of(r0 * GRID_W, GRID_W)
            kb = kl_ref[pl.ds(start, NA_ROWS * GRID_W), :]
            vb = vl_ref[pl.ds(start, NA_ROWS * GRID_W), :]
            s_l = _dot_nt(qs, kb) * scale + bt_ref[dr0].reshape(NA_HEADS * GRID_W, NA_ROWS * GRID_W)
            m = jnp.maximum(m, jnp.max(s_l, -1, keepdims=True))
            p_l = jnp.exp(s_l - m)
        p_c = jnp.exp(s_c - m)
        den = jnp.sum(p_c, -1, keepdims=True)
        o = _dot(p_c.astype(BF16), vc)
        if local:
            den = den + jnp.sum(p_l, -1, keepdims=True)
            o = o + _dot(p_l.astype(BF16), vb)
        o = o / den
        acc = jnp.zeros((GRID_W, NA_DIM), F32)
        for h, hm in enumerate(masks):
            acc = jnp.where(hm, o[h * GRID_W:(h + 1) * GRID_W], acc)
        o_ref[...] = acc.astype(o_ref.dtype)

    @pl.when(j < ncq)
    def _():
        heads(False)

    @pl.when(j >= ncq)
    def _():
        heads(True)


def na_attention(lay, qkv, bias_tab):
    b, c, l = lay.b, lay.c, lay.l
    ncq = c // GRID_W
    rows = lay.rows
    assert rows >= NA_ROWS
    qb = lambda bi, j: jnp.where(j < ncq, bi * ncq + j, lay.nc // GRID_W + bi * rows + (j - ncq))
    return pl.pallas_call(
        functools.partial(_na_kernel, ncq=ncq, rows=rows),
        grid=(b, ncq + rows),
        in_specs=[pl.BlockSpec((GRID_W, NA_DIM), lambda bi, j: (qb(bi, j), 0)),
                  pl.BlockSpec((l, NA_DIM), lambda bi, j: (lay.nc // l + bi, 1)),
                  pl.BlockSpec((l, NA_DIM), lambda bi, j: (lay.nc // l + bi, 2)),
                  pl.BlockSpec((c, NA_DIM), lambda bi, j: (bi, 1)),
                  pl.BlockSpec((c, NA_DIM), lambda bi, j: (bi, 2)),
                  pl.BlockSpec(bias_tab.shape, lambda bi, j: (0, 0, 0, 0))],
        out_specs=pl.BlockSpec((GRID_W, NA_DIM), lambda bi, j: (qb(bi, j), 0)),
        out_shape=jax.ShapeDtypeStruct((lay.t, NA_DIM), BF16),
        compiler_params=_cp("parallel", "arbitrary"),
        name="na_attention",
    )(qkv, qkv, qkv, qkv, qkv, bias_tab)


def na_bias_table(rpb):
    kw = NA_COLS
    col_start = np.clip(np.arange(GRID_W) - kw // 2, 0, GRID_W - kw)
    kc = np.arange(GRID_W)[None, :]
    jq = np.arange(GRID_W)[:, None]
    inside = (kc >= col_start[:, None]) & (kc < col_start[:, None] + kw)
    dc = np.clip(kc - jq + (NA_COLS - 1), 0, 2 * NA_COLS - 2)
    planes = jnp.where(inside[None, None], rpb[:, :, dc], NEG)
    tabs = []
    for dr0 in range(NA_ROWS):
        sel = planes[:, dr0:dr0 + NA_ROWS]
        tabs.append(jnp.transpose(sel, (0, 2, 1, 3)).reshape(NA_HEADS, GRID_W, NA_ROWS * GRID_W))
    return jnp.stack(tabs, 0).astype(F32)


def _conv_kernel(x_ref, p_ref, n_ref, w_ref, b_ref, o_ref, *, nct, tpc, tpl):
    i = pl.program_id(0)
    first = jnp.where(i < nct, i % tpc == 0, (i - nct) % tpl == 0)
    last = jnp.where(i < nct, i % tpc == tpc - 1, (i - nct) % tpl == tpl - 1)
    half = SSD_CONV // 2
    for c0 in range(0, XBC_DIM, 512):
        cs = slice(c0, c0 + 512)
        prev = jnp.where(first, 0.0, p_ref[:, cs])
        nxt = jnp.where(last, 0.0, n_ref[:, cs])
        ext = jnp.concatenate([prev, x_ref[:, cs], nxt], axis=0)
        n_ext = TM + 16
        acc = jnp.zeros((TM, 512), F32) + b_ref[:, cs]
        for k in range(SSD_CONV):
            shifted = ext if k == half else pltpu.roll(ext, (half - k) % n_ext, 0)
            acc = acc + w_ref[k:k + 1, cs] * shifted[8:8 + TM]
        o_ref[:, cs] = _silu(acc)


def conv_silu(lay, xbc, w8, bias):
    n8 = lay.t // 8
    return pl.pallas_call(
        functools.partial(_conv_kernel, nct=lay.nct, tpc=lay.c // TM, tpl=lay.tpb),
        grid=(lay.nt,),
        in_specs=[pl.BlockSpec((TM, XBC_DIM), lambda i: (i, 0)),
                  pl.BlockSpec((8, XBC_DIM), lambda i: (jnp.maximum(i * (TM // 8) - 1, 0), 0)),
                  pl.BlockSpec((8, XBC_DIM), lambda i: (jnp.minimum((i + 1) * (TM // 8), n8 - 1), 0)),
                  pl.BlockSpec((8, XBC_DIM), lambda i: (0, 0)),
                  pl.BlockSpec((1, XBC_DIM), lambda i: (0, 0))],
        out_specs=pl.BlockSpec((TM, XBC_DIM), lambda i: (i, 0)),
        out_shape=jax.ShapeDtypeStruct((lay.t, XBC_DIM), F32),
        compiler_params=_cp("parallel"),
        name="conv_silu",
    )(xbc, xbc, xbc, w8, bias)


def _ssd_kernel(*refs, rev):
    if rev:
        (u_ref, dt_ref, dtt_ref, bn_ref, an_ref, bc_ref, ac_ref, e_ref,
         yf_ref, z_ref, dsk_ref, gn_ref, o_ref, state, ybuf) = refs
    else:
        u_ref, dt_ref, dtt_ref, bn_ref, an_ref, bc_ref, ac_ref, e_ref, o_ref, state, ybuf = refs
    qn = SSD_CHUNK

    @pl.when(pl.program_id(1) == 0)
    def _():
        state[...] = jnp.zeros_like(state)

    xs = u_ref[:, :SSD_INNER]
    dt = _softplus(dt_ref[...] + bn_ref[...])
    a = dt * an_ref[...]
    dt_t = _softplus(dtt_ref[...] + bc_ref[...])
    a_t = dt_t * ac_ref[...]
    li = lax.broadcasted_iota(jnp.int32, (qn, qn), 0)
    si = lax.broadcasted_iota(jnp.int32, (qn, qn), 1)
    keep = (si >= li) if rev else (si <= li)
    tri = keep.astype(BF16)
    tri_t = ((li >= si) if rev else (li <= si)).astype(BF16)
    a_cs = _dot3_r(tri, a)
    a_cs_t = _dot3_l(a_t, tri_t)
    e = e_ref[...]
    dt_x = _dot3_l(dt, e)
    acs_x = _dot3_l(a_cs, e)
    end = 0 if rev else qn - 1
    acs_end = acs_x[end:end + 1, :]
    xdt = xs * dt_x
    xdt_b = xdt.astype(BF16)
    xdte = (xdt * jnp.exp(acs_end - acs_x)).astype(BF16)
    ea_x = jnp.exp(acs_x)
    cd_x = jnp.exp(acs_end)
    lane = lax.broadcasted_iota(jnp.int32, (1, LANE), 1)
    lo = lane < SSD_HEAD_DIM
    cb = None
    for p in range(SSD_HEADS // 2):
        g = p // 2
        ps = slice(p * LANE, (p + 1) * LANE)
        if p % 2 == 0:
            bg = u_ref[:, SSD_INNER + g * SSD_STATE:SSD_INNER + (g + 1) * SSD_STATE]
            cg = u_ref[:, SSD_INNER + SSD_BC + g * SSD_STATE:SSD_INNER + SSD_BC + (g + 1) * SSD_STATE].astype(BF16)
            cb = _dot_nt(cg, bg.astype(BF16))
            bg_t = bg.T.astype(BF16)
        xp = xdt_b[:, ps]
        y = None
        for hh in range(2):
            h = 2 * p + hh
            seg = a_cs[:, h:h + 1] - a_cs_t[h:h + 1, :]
            m = (jnp.exp(jnp.where(keep, seg, NEG)) * cb).astype(BF16)
            xh = jnp.where(lo if hh == 0 else jnp.logical_not(lo), xp, jnp.zeros_like(xp))
            yh = _dot(m, xh)
            y = yh if y is None else y + yh
        st = state[p]
        y = y + _dot(cg, st.astype(BF16)) * ea_x[:, ps]
        state[p] = cd_x[:, ps] * st + _dot(bg_t, xdte[:, ps])
        ybuf[:, ps] = y
    if rev:
        y = dsk_ref[...] * xs + yf_ref[...] + ybuf[...]
        y = y * _silu(z_ref[...])
        o_ref[...] = (y * lax.rsqrt(jnp.mean(y * y, -1, keepdims=True) + NORM_EPS) * gn_ref[...]).astype(o_ref.dtype)
    else:
        o_ref[...] = ybuf[...]


def ssd_pass(lay, rev, u, dt, dtt, bias_nat, a_nat, bias_col, a_col, expand, extra=None):
    qn = SSD_CHUNK
    ncc, ncl = lay.c // qn, lay.l // qn
    nctq = lay.nc // qn
    d = 1 if rev else 0

    def chunk(bi, j):
        jc = (ncc - 1 - j) if rev else j
        jl = (ncl - 1 - (j - ncc)) if rev else (j - ncc)
        return jnp.where(j < ncc, bi * ncc + jc, nctq + bi * ncl + jl)

    row = lambda n: pl.BlockSpec((1, n), lambda bi, j: (0, 0))
    in_specs = [pl.BlockSpec((qn, XBC_DIM), lambda bi, j: (chunk(bi, j), 0)),
                pl.BlockSpec((qn, LANE), lambda bi, j: (chunk(bi, j), d)),
                pl.BlockSpec((SSD_HEADS, qn), lambda bi, j: (d, chunk(bi, j))),
                row(LANE), row(LANE),
                pl.BlockSpec((SSD_HEADS, 1), lambda bi, j: (0, 0)), pl.BlockSpec((SSD_HEADS, 1), lambda bi, j: (0, 0)),
                pl.BlockSpec((LANE, SSD_INNER), lambda bi, j: (0, 0))]
    args = [u, dt, dtt, bias_nat, a_nat, bias_col, a_col, expand]
    if rev:
        yf, z, dsk, gn = extra
        in_specs += [pl.BlockSpec((qn, SSD_INNER), lambda bi, j: (chunk(bi, j), 0)),
                     pl.BlockSpec((qn, SSD_INNER), lambda bi, j: (chunk(bi, j), 0)),
                     row(SSD_INNER), row(SSD_INNER)]
        args += [yf, z, dsk, gn]
    return pl.pallas_call(
        functools.partial(_ssd_kernel, rev=rev),
        grid=(lay.b, ncc + ncl),
        in_specs=in_specs,
        out_specs=pl.BlockSpec((qn, SSD_INNER), lambda bi, j: (chunk(bi, j), 0)),
        out_shape=jax.ShapeDtypeStruct((lay.t, SSD_INNER), BF16 if rev else F32),
        scratch_shapes=[pltpu.VMEM((SSD_HEADS // 2, SSD_STATE, LANE), F32), pltpu.VMEM((qn, SSD_INNER), F32)],
        compiler_params=_cp("parallel", "arbitrary"),
        name="ssd_bwd" if rev else "ssd_fwd",
    )(*args)


def _rope_tables(lay):
    t = np.arange(lay.l)
    pos_r, pos_c = t // GRID_W, t % GRID_W

    def half_tables(half):
        inv = ROPE_THETA ** (-np.arange(0, half, 2, dtype=np.float32) / half)
        cos, sin = [], []
        for pos in (pos_r, pos_c):
            ang = pos.astype(np.float32)[:, None] * inv[None, :]
            cos += [np.cos(ang), np.cos(ang)]
            sin += [-np.sin(ang), np.sin(ang)]
        return np.concatenate(cos, 1), np.concatenate(sin, 1)

    c32, s32 = half_tables(MLA_ROPE // 2)
    c64, s64 = half_tables(GQA_HEAD_DIM // 2)
    mla_c = np.concatenate([np.ones((lay.l, MLA_NOPE)), c32, np.zeros((lay.l, LANE - MLA_NOPE - MLA_ROPE))], 1)
    mla_s = np.concatenate([np.zeros((lay.l, MLA_NOPE)), s32, np.zeros((lay.l, LANE - MLA_NOPE - MLA_ROPE))], 1)
    ident_c = np.concatenate([np.ones((TM, MLA_NOPE + MLA_ROPE)), np.zeros((TM, LANE - MLA_NOPE - MLA_ROPE))], 1)
    gqa_c = np.concatenate([c64, c64], 1)
    gqa_s = np.concatenate([s64, s64], 1)
    f = lambda ident, tab: jnp.asarray(np.concatenate([ident, tab], 0), F32)
    return (f(ident_c, mla_c), f(np.zeros((TM, LANE)), mla_s),
            f(np.ones((TM, LANE)), gqa_c), f(np.zeros((TM, LANE)), gqa_s))


def _prep_kernel(cq_ref, ckv_ref, gq_ref, gqp_ref, gk_ref, gkp_ref, kra_ref, krb_ref, gv_ref,
                 qn_ref, kvn_ref, gqn_ref, gqnp_ref, gkn_ref, gknp_ref,
                 wq_ref, wqp_ref, wk_ref, wv_ref, bd_ref, mc_ref, ms_ref, gc_ref, gs_ref,
                 mq_ref, mk_ref, mv_ref, oq_ref, ok_ref, ov_ref):
    def rms(x, g):
        return x * lax.rsqrt(jnp.mean(x * x, -1, keepdims=True) + NORM_EPS) * g

    mc, ms = mc_ref[...], ms_ref[...]
    mc8 = jnp.concatenate([mc] * MLA_HEADS, 1)
    ms8 = jnp.concatenate([ms] * MLA_HEADS, 1)
    rq = rms(cq_ref[...], qn_ref[...]).astype(BF16)
    q = _dot(rq, wq_ref[...]) * mc8 + _dot(rq, wqp_ref[...]) * ms8
    mq_ref[...] = (q * MLA_SCALE).astype(BF16)
    rkv = rms(ckv_ref[...], kvn_ref[...]).astype(BF16)
    kr = kra_ref[...] * mc + krb_ref[...] * ms
    mk_ref[...] = (_dot(rkv, wk_ref[...]) + jnp.concatenate([kr] * MLA_HEADS, 1)).astype(BF16)
    one_lane = lax.broadcasted_iota(jnp.int32, (1, D_MODEL), 1) % LANE == V_ONE_LANE
    mv_ref[...] = jnp.where(one_lane, 1.0, _dot(rkv, wv_ref[...])).astype(BF16)
    ov_ref[...] = jnp.where(one_lane[:, :2 * LANE], jnp.ones((), BF16), gv_ref[...])
    gc4 = jnp.concatenate([gc_ref[...]] * 4, 1)
    gs4 = jnp.concatenate([gs_ref[...]] * 4, 1)
    bd = bd_ref[...]

    def qk_norm_rope(u, up, g, gp, scale):
        sq = u * u
        hi = sq.astype(BF16)
        lo = (sq - hi.astype(F32)).astype(BF16)
        r = lax.rsqrt(_dot(hi, bd) + _dot(lo, bd) + NORM_EPS)
        return ((u * g * gc4 + up * gp * gs4) * (r * scale)).astype(BF16)

    oq_ref[...] = qk_norm_rope(gq_ref[...], gqp_ref[...], gqn_ref[...], gqnp_ref[...], GQA_SCALE)
    ok_ref[...] = qk_norm_rope(gk_ref[...], gkp_ref[...], gkn_ref[...], gknp_ref[...], 1.0)


def odd_prep(lay, acts, norms, weights, tables):
    tab_row = lambda i: jnp.where(i < lay.nct, 0, 1 + (i - lay.nct) % lay.tpb)
    tok = lambda n: pl.BlockSpec((TM, n), lambda i: (i, 0))
    const = lambda a: pl.BlockSpec(a.shape, lambda i: (0, 0))
    in_specs = [tok(a.shape[1]) for a in acts] + [const(a) for a in norms] + [const(a) for a in weights]
    in_specs += [pl.BlockSpec((TM, LANE), lambda i: (tab_row(i), 0)) for _ in tables]
    widths = (D_MODEL, D_MODEL, D_MODEL, 512, 512, 2 * LANE)
    return pl.pallas_call(
        _prep_kernel,
        grid=(lay.nt,),
        in_specs=in_specs,
        out_specs=[tok(n) for n in widths],
        out_shape=[jax.ShapeDtypeStruct((lay.t, n), BF16) for n in widths],
        compiler_params=_cp("parallel"),
        name="odd_prep",
    )(*acts, *norms, *weights, *tables)


def _lane_block_max(s):
    m = s[:, :LANE]
    for c0 in range(LANE, s.shape[1], LANE):
        m = jnp.maximum(m, s[:, c0:c0 + LANE])
    return m


def _attn_kernel(q_ref, kc_ref, kl_ref, vc_ref, vl_ref, o_ref, *, ncq, q_shared, v_shared):
    j = pl.program_id(2)
    lane = lax.broadcasted_iota(jnp.int32, (1, LANE), 1)

    def run(with_lat):
        outs = []
        for hh in range(2):
            q = q_ref[...] if q_shared else q_ref[:, hh * LANE:(hh + 1) * LANE]
            hs = slice(hh * LANE, (hh + 1) * LANE)
            vs = slice(0, LANE) if v_shared else hs
            s_c = _dot_nt(q, kc_ref[:, hs]).astype(BF16)
            m = _lane_block_max(s_c)
            if with_lat:
                s_l = _dot_nt(q, kl_ref[:, hs]).astype(BF16)
                m = jnp.maximum(m, _lane_block_max(s_l))
            m = jnp.max(m.astype(F32), -1, keepdims=True).astype(BF16)
            o = _dot(jnp.exp(s_c - m), vc_ref[:, vs])
            if with_lat:
                o = o + _dot(jnp.exp(s_l - m), vl_ref[:, vs])
            outs.append(o / o[:, V_ONE_LANE:V_ONE_LANE + 1])
        out = jnp.where(lane < V_ONE_LANE, outs[0], pltpu.roll(outs[1], V_ONE_LANE, 1))
        o_ref[...] = out.astype(o_ref.dtype)

    @pl.when(j < ncq)
    def _():
        run(False)

    @pl.when(j >= ncq)
    def _():
        run(True)


def dense_attention(lay, q, k, v, shared, kv_block):
    b, c, l = lay.b, lay.c, lay.l
    tq = ATT_TQ
    ncq, nlq = c // tq, l // tq
    qw = LANE if shared else 2 * LANE
    qb = lambda bi, j: jnp.where(j < ncq, bi * ncq + j, lay.nc // tq + bi * nlq + (j - ncq))
    ctx_spec = pl.BlockSpec((c, 2 * LANE), lambda bi, p, j: (bi, kv_block(p)))
    lat_spec = pl.BlockSpec((l, 2 * LANE), lambda bi, p, j: (lay.nc // l + bi, kv_block(p)))
    ctx_v = pl.BlockSpec((c, qw), lambda bi, p, j: (bi, kv_block(p)))
    lat_v = pl.BlockSpec((l, qw), lambda bi, p, j: (lay.nc // l + bi, kv_block(p)))
    return pl.pallas_call(
        functools.partial(_attn_kernel, ncq=ncq, q_shared=shared, v_shared=shared),
        grid=(b, 4, ncq + nlq),
        in_specs=[pl.BlockSpec((tq, qw), lambda bi, p, j: (qb(bi, j), p)), ctx_spec, lat_spec, ctx_v, lat_v],
        out_specs=pl.BlockSpec((tq, LANE), lambda bi, p, j: (qb(bi, j), p)),
        out_shape=jax.ShapeDtypeStruct((lay.t, 4 * LANE), BF16),
        compiler_params=_cp("parallel", "parallel", "arbitrary"),
        name="dense_attention",
    )(q, k, k, v, v)


GATHER_UNROLL = 8


def _moe_ffn_kernel(be_ref, nb_ref, tok_hbm, h_hbm, sw_ref, w1_ref, w3_ref, w2_ref, y_ref, idx, xbuf, sem_i, sem_g):
    i = pl.program_id(0)
    n_used = nb_ref[0]

    def idx_copy(blk):
        slot = blk % 2
        return pltpu.make_async_copy(tok_hbm.at[pl.ds(blk * MOE_TM, MOE_TM)], idx.at[slot], sem_i.at[slot])

    def gather(blk):
        slot = blk % 2

        def issue(r, carry):
            pltpu.make_async_copy(h_hbm.at[pl.ds(idx[slot, r], 1), :], xbuf.at[slot, pl.ds(r, 1), :],
                                  sem_g.at[slot]).start()
            return carry

        lax.fori_loop(0, MOE_TM, issue, 0, unroll=GATHER_UNROLL)

    @pl.when((i == 0) & (n_used > 0))
    def _():
        idx_copy(0).start()
        idx_copy(0).wait()
        gather(0)

        @pl.when(n_used > 1)
        def _():
            idx_copy(1).start()

    @pl.when(i + 1 < n_used)
    def _():
        idx_copy(i + 1).wait()
        gather(i + 1)

    @pl.when(i + 2 < n_used)
    def _():
        idx_copy(i + 2).start()

    @pl.when(i < n_used)
    def _():
        slot = i % 2
        pltpu.make_async_copy(h_hbm.at[pl.ds(0, MOE_TM), :], xbuf.at[slot], sem_g.at[slot]).wait()
        x = xbuf[slot].astype(BF16)
        hid = (_silu(_dot(x, w1_ref[0])) * _dot(x, w3_ref[0])).astype(BF16)
        y_ref[...] = _dot(hid, w2_ref[0]) * sw_ref[...]

    @pl.when(i >= n_used)
    def _():
        y_ref[...] = jnp.zeros_like(y_ref)


def moe_ffn(h, blk_e, n_used, slot_tok, slot_w, w1, w3, w2):
    nb = blk_e.shape[0]
    wspec = lambda shape: pl.BlockSpec((1,) + shape, lambda i, be, nu: (be[i], 0, 0))
    return pl.pallas_call(
        _moe_ffn_kernel,
        grid_spec=pltpu.PrefetchScalarGridSpec(
            num_scalar_prefetch=2,
            grid=(nb,),
            in_specs=[pl.BlockSpec(memory_space=pl.ANY), pl.BlockSpec(memory_space=pl.ANY),
                      pl.BlockSpec((MOE_TM, 1), lambda i, be, nu: (i, 0)),
                      wspec((D_MODEL, MOE_HIDDEN)), wspec((D_MODEL, MOE_HIDDEN)), wspec((MOE_HIDDEN, D_MODEL))],
            out_specs=pl.BlockSpec((MOE_TM, D_MODEL), lambda i, be, nu: (i, 0)),
            scratch_shapes=[pltpu.SMEM((2, MOE_TM), jnp.int32), pltpu.VMEM((2, MOE_TM, D_MODEL), F32),
                            pltpu.SemaphoreType.DMA((2,)), pltpu.SemaphoreType.DMA((2,))]),
        out_shape=jax.ShapeDtypeStruct((nb * MOE_TM, D_MODEL), F32),
        compiler_params=_cp("arbitrary"),
        name="moe_ffn",
    )(blk_e, n_used, slot_tok, h, slot_w, w1, w3, w2)


def _combine_kernel(pos_hbm, y_hbm, x_ref, gate_ref, g_ref, b_ref, o_ref, idx, buf, sem_i, sem_g, *, alpha):
    i = pl.program_id(0)
    n = pl.num_programs(0)

    def idx_copy(tile):
        slot = tile % 2
        return pltpu.make_async_copy(pos_hbm.at[pl.ds(tile * (MOE_TOPK * TM), MOE_TOPK * TM)], idx.at[slot],
                                     sem_i.at[slot])

    def gather(tile):
        slot = tile % 2

        def issue(r, carry):
            for k in range(MOE_TOPK):
                pltpu.make_async_copy(y_hbm.at[pl.ds(idx[slot, MOE_TOPK * r + k], 1), :],
                                      buf.at[slot, k, pl.ds(r, 1), :], sem_g.at[slot]).start()
            return carry

        lax.fori_loop(0, TM, issue, 0, unroll=GATHER_UNROLL // MOE_TOPK)

    @pl.when(i == 0)
    def _():
        idx_copy(0).start()
        idx_copy(0).wait()
        gather(0)

        @pl.when(n > 1)
        def _():
            idx_copy(1).start()

    @pl.when(i + 1 < n)
    def _():
        idx_copy(i + 1).wait()
        gather(i + 1)

    @pl.when(i + 2 < n)
    def _():
        idx_copy(i + 2).start()

    slot = i % 2
    for k in range(MOE_TOPK):
        pltpu.make_async_copy(y_hbm.at[pl.ds(0, TM), :], buf.at[slot, k], sem_g.at[slot]).wait()
    f = buf[slot, 0] + buf[slot, 1]
    o_ref[...] = _layer_norm(alpha * x_ref[...] + gate_ref[0] * f, g_ref[...], b_ref[...])


def moe_combine_ln(lay, pos, y, x, gate, ln_g, ln_b, alpha):
    row_spec = pl.BlockSpec((1, D_MODEL), lambda i: (0, 0))
    return pl.pallas_call(
        functools.partial(_combine_kernel, alpha=alpha),
        grid=(lay.nt,),
        in_specs=[pl.BlockSpec(memory_space=pl.ANY), pl.BlockSpec(memory_space=pl.ANY),
                  pl.BlockSpec((TM, D_MODEL), lambda i: (i, 0)),
                  pl.BlockSpec((1, 1, D_MODEL), lambda i: (lay.mod_row(i), 0, 0)), row_spec, row_spec],
        out_specs=pl.BlockSpec((TM, D_MODEL), lambda i: (i, 0)),
        out_shape=jax.ShapeDtypeStruct((lay.t, D_MODEL), F32),
        scratch_shapes=[pltpu.SMEM((2, MOE_TOPK * TM), jnp.int32), pltpu.VMEM((2, MOE_TOPK, TM, D_MODEL), F32),
                        pltpu.SemaphoreType.DMA((2,)), pltpu.SemaphoreType.DMA((2,))],
        compiler_params=_cp("arbitrary"),
        name="moe_combine_ln",
    )(pos, y, x, gate, ln_g, ln_b)


def moe_route(logits):
    t = logits.shape[0]
    g_logit = logits[:, :MOE_GROUPS]
    g_sel = jnp.argmax(g_logit, axis=-1).astype(jnp.int32)
    g_gate = 1.0 / jnp.sum(jnp.exp(g_logit - jnp.max(g_logit, -1, keepdims=True)), -1)
    e_logits = logits[:, MOE_GROUPS:MOE_GROUPS + MOE_EXPERTS].reshape(t, MOE_GROUPS, MOE_EPG)
    g_hot = g_sel[:, None] == jnp.arange(MOE_GROUPS, dtype=jnp.int32)[None, :]
    e_in = jnp.sum(jnp.where(g_hot[:, :, None], e_logits, 0.0), axis=1)
    lanes = jnp.arange(MOE_EPG, dtype=jnp.int32)[None, :]
    i1 = jnp.argmax(e_in, axis=-1).astype(jnp.int32)
    v1 = jnp.max(e_in, axis=-1)
    rest = jnp.where(lanes == i1[:, None], -jnp.inf, e_in)
    i2 = jnp.argmax(rest, axis=-1).astype(jnp.int32)
    v2 = jnp.max(rest, axis=-1)
    p2 = jnp.exp(v2 - v1)
    gate = g_gate[:, None] * jnp.stack([1.0 / (1.0 + p2), p2 / (1.0 + p2)], -1)
    expert = g_sel[:, None] * MOE_EPG + jnp.stack([i1, i2], -1)
    return expert.astype(jnp.int32), gate


def moe_dispatch_plan(expert, gate):
    t = expert.shape[0]
    n_assign = t * MOE_TOPK
    flat_e = expert.reshape(n_assign)
    flat_w = gate.reshape(n_assign)
    eids = jnp.arange(MOE_EXPERTS, dtype=jnp.int32)
    order = jnp.argsort(flat_e).astype(jnp.int32)
    rank = jnp.argsort(order).astype(jnp.int32)
    hot = flat_e[:, None] == eids[None, :]
    counts = jnp.sum(hot, axis=0, dtype=jnp.int32)
    starts = jnp.cumsum(counts) - counts
    pcounts = (counts + MOE_TM - 1) // MOE_TM * MOE_TM
    pends = jnp.cumsum(pcounts)
    pstarts = pends - pcounts
    nb = (n_assign + MOE_EXPERTS * (MOE_TM - 1)) // MOE_TM
    blk_start = jnp.arange(nb, dtype=jnp.int32) * MOE_TM
    blk_e = jnp.minimum(jnp.sum(pends[None, :] <= blk_start[:, None], axis=1), MOE_EXPERTS - 1).astype(jnp.int32)
    n_used = (pends[-1] // MOE_TM).astype(jnp.int32).reshape(1)
    blk_hot = blk_e[:, None] == eids[None, :]
    pick = lambda table: jnp.sum(jnp.where(blk_hot, table[None, :], 0), axis=1)[:, None]
    in_blk = jnp.arange(MOE_TM, dtype=jnp.int32)[None, :]
    seg_rank = blk_start[:, None] + in_blk - pick(pstarts)
    valid = (seg_rank < pick(counts)).reshape(-1)
    src = jnp.clip(pick(starts) + seg_rank, 0, n_assign - 1)
    assign = order[src.reshape(-1)]
    slot_tok = jnp.where(valid, assign // MOE_TOPK, 0).astype(jnp.int32)
    slot_w = jnp.where(valid, flat_w[assign], 0.0).astype(F32)
    pos = (rank + jnp.sum(jnp.where(hot, (pstarts - starts)[None, :], 0), axis=1)).astype(jnp.int32)
    return blk_e, n_used, slot_tok, slot_w.reshape(-1, 1), pos


def _rope_partner(dim):
    q = dim // 4
    idx = np.arange(dim)
    return np.where((idx // q) % 2 == 0, idx + q, idx - q)


def _even_weights(w_in, w_out, conv_w, conv_b, dt_bias, a_log, d_skip, gnorm_w):
    n_main = NA_DIM * 3 + SSD_INNER + XBC_DIM
    w_dt = jnp.zeros((D_MODEL, 2 * LANE), F32)
    w_dt = w_dt.at[:, :SSD_HEADS].set(w_in[:, n_main:n_main + SSD_HEADS])
    w_dt = w_dt.at[:, LANE:LANE + SSD_HEADS].set(w_in[:, n_main + SSD_HEADS:])
    w = jnp.concatenate([w_in[:, :n_main], w_dt], 1).astype(BF16)
    wt = w_in[:, n_main:].T.astype(BF16)
    pad = lambda v: jnp.zeros((1, LANE), F32).at[0, :SSD_HEADS].set(v)
    a = -jnp.exp(a_log.astype(F32))
    expand = np.zeros((LANE, SSD_INNER), np.float32)
    for h in range(SSD_HEADS):
        expand[h, h * SSD_HEAD_DIM:(h + 1) * SSD_HEAD_DIM] = 1.0
    return dict(
        w=w, wt=wt,
        conv_w=jnp.zeros((8, XBC_DIM), F32).at[:SSD_CONV].set(conv_w), conv_b=conv_b.reshape(1, XBC_DIM),
        bias_nat=[pad(dt_bias[d]) for d in range(2)], a_nat=[pad(a[d]) for d in range(2)],
        bias_col=[dt_bias[d].reshape(SSD_HEADS, 1) for d in range(2)], a_col=[a[d].reshape(SSD_HEADS, 1) for d in range(2)],
        expand=jnp.asarray(expand, BF16),
        dskip=jnp.repeat(d_skip, SSD_HEAD_DIM).reshape(1, SSD_INNER), gnorm=gnorm_w.reshape(1, SSD_INNER),
        w_out=[w_out[:NA_DIM].astype(BF16), w_out[NA_DIM:].astype(BF16)])


def _odd_weights(w_in, w_out, q_norm, kv_norm, w_uq, w_ukv, gq_norm, gk_norm):
    o = np.cumsum([0, MLA_Q_LORA, GQA_HEADS * GQA_HEAD_DIM, MLA_KV_LORA, MLA_ROPE, 128, 128])
    w_cq, w_gq, w_ckv, w_kr, w_gk, w_gv = [w_in[:, o[i]:o[i + 1]] for i in range(6)]
    p64, p32 = _rope_partner(GQA_HEAD_DIM), _rope_partner(MLA_ROPE)
    head_perm = lambda n: np.concatenate([h * GQA_HEAD_DIM + p64 for h in range(n)])
    z64 = jnp.zeros((D_MODEL, GQA_HEAD_DIM), F32)

    def placed(wm):
        out = []
        for g in range(GQA_KV_HEADS):
            wg = wm[:, g * GQA_HEAD_DIM:(g + 1) * GQA_HEAD_DIM]
            out += [wg, z64, z64, wg]
        return jnp.concatenate(out, 1)

    kr_slab = lambda wm: jnp.zeros((D_MODEL, LANE), F32).at[:, MLA_NOPE:MLA_NOPE + MLA_ROPE].set(wm)
    w = jnp.concatenate([w_cq, w_ckv, w_gq, w_gq[:, head_perm(GQA_HEADS)],
                         placed(w_gk), placed(w_gk[:, head_perm(GQA_KV_HEADS)]),
                         kr_slab(w_kr), kr_slab(w_kr[:, p32]),
                         w_gv[:, :GQA_HEAD_DIM], z64, w_gv[:, GQA_HEAD_DIM:], z64], 1).astype(BF16)
    splits = [(MLA_Q_LORA, F32), (MLA_KV_LORA, F32), (512, F32), (512, F32), (512, F32), (512, F32),
              (LANE, F32), (LANE, F32), (2 * LANE, BF16)]
    uq = w_uq.reshape(MLA_Q_LORA, MLA_HEADS, MLA_NOPE + MLA_ROPE)
    pad_q = jnp.zeros((MLA_Q_LORA, MLA_HEADS, LANE - MLA_NOPE - MLA_ROPE), F32)
    wq = jnp.concatenate([uq, pad_q], 2).reshape(MLA_Q_LORA, D_MODEL)
    uq_par = jnp.concatenate([jnp.zeros_like(uq[:, :, :MLA_NOPE]), uq[:, :, MLA_NOPE:][:, :, p32]], 2)
    wqp = jnp.concatenate([uq_par, pad_q], 2).reshape(MLA_Q_LORA, D_MODEL)
    ukv = w_ukv.reshape(MLA_KV_LORA, MLA_HEADS, MLA_NOPE + MLA_V)
    zk = jnp.zeros((MLA_KV_LORA, MLA_HEADS, MLA_NOPE), F32)
    wk = jnp.concatenate([ukv[:, :, :MLA_NOPE], zk], 2).reshape(MLA_KV_LORA, D_MODEL)
    uv = ukv[:, :, MLA_NOPE:]
    wv = jnp.concatenate([uv, jnp.zeros_like(uv)], 2).reshape(MLA_KV_LORA, D_MODEL)
    bd = np.kron(np.eye(8, dtype=np.float32), np.full((64, 64), 1.0 / 64, np.float32))
    tile = lambda v, n: jnp.tile(v, n).reshape(1, -1)
    return dict(
        w=w, splits=splits,
        norms=[q_norm.reshape(1, -1), kv_norm.reshape(1, -1), tile(gq_norm, 8), tile(gq_norm[p64], 8),
               tile(gk_norm, 8), tile(gk_norm[p64], 8)],
        weights=[wq.astype(BF16), wqp.astype(BF16), wk.astype(BF16), wv.astype(BF16), jnp.asarray(bd, BF16)],
        w_out=[w_out[:512].astype(BF16), w_out[512:].astype(BF16)])


def kernel(x, c, ctx, c_ctx, ada_w, ada_b, ln_g, ln_b, e_w_in, e_w_out, e_conv_w, e_conv_b, e_dt_bias, e_a_log, e_d_skip, e_gnorm_w, e_rpb, o_w_in, o_w_out, o_mla_q_norm, o_mla_kv_norm, o_w_uq, o_w_ukv, o_gqa_q_norm, o_gqa_k_norm, moe_wg, moe_bg, moe_we, moe_be, moe_w1, moe_w3, moe_w2):
    b, l, dm = x.shape
    cl = ctx.shape[1]
    depth = ada_w.shape[0]
    lay = _Layout(b, cl, l)
    alpha = (2.0 * depth) ** 0.25
    xa = jnp.concatenate([ctx.reshape(b * cl, dm), x.reshape(b * l, dm)], 0)
    cvec = jnp.zeros((MOD_ROWS, dm), F32).at[:b].set(c).at[b].set(c_ctx)
    mods = ada_mod(cvec, ada_w, ada_b).reshape(depth, MOD_ROWS, N_MOD, 1, dm).transpose(0, 2, 1, 3, 4)
    rope_tabs = _rope_tables(lay)
    row = lambda v: v.reshape(1, -1)
    for layer in range(depth):
        i = layer // 2
        mod = [mods[layer, k] for k in range(N_MOD)]
        if layer % 2 == 0:
            ew = _even_weights(e_w_in[i], e_w_out[i], e_conv_w[i], e_conv_b[i], e_dt_bias[i], e_a_log[i],
                               e_d_skip[i], e_gnorm_w[i])
            qkv, z, xbc, dt, dtt = fused_proj(
                lay, xa, mod[0], mod[1], ew['w'],
                [(3 * NA_DIM, BF16), (SSD_INNER, F32), (XBC_DIM, F32), (2 * LANE, F32)], wt=ew['wt'])
            na = na_attention(lay, qkv, na_bias_table(e_rpb[i]))
            u = conv_silu(lay, xbc, ew['conv_w'], ew['conv_b'])
            yf = ssd_pass(lay, False, u, dt, dtt, ew['bias_nat'][0], ew['a_nat'][0], ew['bias_col'][0],
                          ew['a_col'][0], ew['expand'])
            ssd = ssd_pass(lay, True, u, dt, dtt, ew['bias_nat'][1], ew['a_nat'][1], ew['bias_col'][1],
                           ew['a_col'][1], ew['expand'], extra=(yf, z, ew['dskip'], ew['gnorm']))
            mixes, w_outs = [na, ssd], ew['w_out']
        else:
            ow = _odd_weights(o_w_in[i], o_w_out[i], o_mla_q_norm[i], o_mla_kv_norm[i], o_w_uq[i], o_w_ukv[i],
                              o_gqa_q_norm[i], o_gqa_k_norm[i])
            cq, ckv, gq, gqp, gk, gkp, kra, krb, gv = fused_proj(lay, xa, mod[0], mod[1], ow['w'], ow['splits'])
            mq, mk, mv, gqr, gkr, gvo = odd_prep(lay, [cq, ckv, gq, gqp, gk, gkp, kra, krb, gv], ow['norms'],
                                                 ow['weights'], rope_tabs)
            o_mla = dense_attention(lay, mq, mk, mv, False, lambda p: p)
            o_gqa = dense_attention(lay, gqr, gkr, gvo, True, lambda p: p // 2)
            mixes, w_outs = [o_mla, o_gqa], ow['w_out']
        wr = jnp.zeros((dm, LANE), F32).at[:, :MOE_GROUPS].set(moe_wg[layer])
        wr = wr.at[:, MOE_GROUPS:MOE_GROUPS + MOE_EXPERTS].set(moe_we[layer])
        wr_hi = wr.astype(BF16)
        wr_lo = (wr - wr_hi.astype(F32)).astype(BF16)
        br = jnp.zeros((1, LANE), F32).at[0, :MOE_GROUPS].set(moe_bg[layer])
        br = br.at[0, MOE_GROUPS:MOE_GROUPS + MOE_EXPERTS].set(moe_be[layer])
        xa, h2, logits = outproj_ln(lay, xa, mixes, w_outs, mod[2], row(ln_g[layer, 0]), row(ln_b[layer, 0]),
                                    mod[3], mod[4], wr_hi, wr_lo, br, alpha)
        expert, gate = moe_route(logits)
        blk_e, n_used, slot_tok, slot_w, pos = moe_dispatch_plan(expert, gate)
        y = moe_ffn(h2, blk_e, n_used, slot_tok, slot_w, moe_w1[layer].astype(BF16), moe_w3[layer].astype(BF16),
                    moe_w2[layer].astype(BF16))
        xa = moe_combine_ln(lay, pos, y, xa, mod[5], row(ln_g[layer, 1]), row(ln_b[layer, 1]), alpha)
    return xa[lay.nc:].reshape(b, l, dm)
```

```python
import functools
import math

import numpy as np
import jax
import jax.numpy as jnp
from jax import lax
from jax.experimental import pallas as pl
from jax.experimental.pallas import tpu as pltpu

F32 = jnp.float32
BF16 = jnp.bfloat16

D_MODEL = 1024
GRID_W = 64
NORM_EPS = 1e-6
ROPE_THETA = 10000.0
N_MOD = 6

SSD_HEADS = 16
SSD_HEAD_DIM = 64
SSD_INNER = 1024
SSD_GROUPS = 4
SSD_STATE = 128
SSD_BC = 512
XBC_DIM = 2048
SSD_CONV = 5
SSD_CHUNK = 128

NA_HEADS = 8
NA_HEAD_DIM = 64
NA_DIM = 512
NA_ROWS = 8
NA_COLS = 16

MLA_HEADS = 8
MLA_Q_LORA = 384
MLA_KV_LORA = 256
MLA_NOPE = 64
MLA_ROPE = 32
MLA_V = 64
MLA_SCALE = (MLA_NOPE + MLA_ROPE) ** -0.5
GQA_HEADS = 8
GQA_KV_HEADS = 2
GQA_HEAD_DIM = 64
GQA_SCALE = GQA_HEAD_DIM ** -0.5

MOE_GROUPS = 4
MOE_EPG = 8
MOE_EXPERTS = 32
MOE_TOPK = 2
MOE_HIDDEN = 512

LANE = 128
TM = 256
MOE_TM = 256
ATT_TQ = 256
V_ONE_LANE = 64
MOD_ROWS = 24
NEG = -1e30
VMEM_LIMIT = 56 * 1024 * 1024


def _cp(*sem):
    return pltpu.CompilerParams(dimension_semantics=sem, vmem_limit_bytes=VMEM_LIMIT)


def _dot(a, b):
    return jnp.dot(a, b, preferred_element_type=F32)


def _dot_nt(a, b):
    return lax.dot_general(a, b, (((1,), (1,)), ((), ())), preferred_element_type=F32)


def _split3(x):
    x1 = x.astype(BF16)
    r = x - x1.astype(F32)
    x2 = r.astype(BF16)
    r = r - x2.astype(F32)
    return x1, x2, r.astype(BF16)


def _dot3_l(x, m):
    a, b, c = _split3(x)
    return _dot(a, m) + _dot(b, m) + _dot(c, m)


def _dot3_r(m, x):
    a, b, c = _split3(x)
    return _dot(m, a) + _dot(m, b) + _dot(m, c)


def _silu(x):
    return x * jax.nn.sigmoid(x)


def _softplus(x):
    return jnp.maximum(x, 0.0) + jnp.log1p(jnp.exp(-jnp.abs(x)))


def _layer_norm(v, g, b):
    mu = jnp.mean(v, -1, keepdims=True)
    vc = v - mu
    var = jnp.mean(vc * vc, -1, keepdims=True)
    return vc * lax.rsqrt(var + NORM_EPS) * g + b


class _Layout:
    def __init__(self, b, c, l):
        assert c % TM == 0 and l % TM == 0 and (b * c) % l == 0 and l % GRID_W == 0
        assert b + 1 <= MOD_ROWS
        self.b, self.c, self.l = b, c, l
        self.nc = b * c
        self.t = b * c + b * l
        self.nct = self.nc // TM
        self.nt = self.t // TM
        self.tpb = l // TM
        self.rows = l // GRID_W

    def mod_row(self, i):
        return jnp.where(i < self.nct, self.b, (i - self.nct) // self.tpb)


def _ada_kernel(c_ref, w_ref, b_ref, o_ref):
    s = _silu(c_ref[...]).astype(BF16)
    o_ref[0] = _dot(s, w_ref[0].astype(BF16)) + b_ref[0]


def ada_mod(cvec, ada_w, ada_b):
    depth, dm, n = ada_w.shape
    tn = 1536
    assert n % tn == 0
    return pl.pallas_call(
        _ada_kernel,
        grid=(depth, n // tn),
        in_specs=[pl.BlockSpec((MOD_ROWS, dm), lambda l, j: (0, 0)),
                  pl.BlockSpec((1, dm, tn), lambda l, j: (l, 0, j)),
                  pl.BlockSpec((1, 1, tn), lambda l, j: (l, 0, j))],
        out_specs=pl.BlockSpec((1, MOD_ROWS, tn), lambda l, j: (l, 0, j)),
        out_shape=jax.ShapeDtypeStruct((depth, MOD_ROWS, n), F32),
        compiler_params=_cp("arbitrary", "arbitrary"),
        name="ada_mod",
    )(cvec, ada_w, ada_b.reshape(depth, 1, n))


def _proj_kernel(*refs, splits, with_t):
    x_ref, sh_ref, sc_ref, w_ref = refs[:4]
    pos = 4
    wt_ref = None
    if with_t:
        wt_ref = refs[pos]
        pos += 1
    outs = refs[pos:]
    xm = (x_ref[...] * (1.0 + sc_ref[0]) + sh_ref[0]).astype(BF16)
    off = 0
    for o_ref, (n, _) in zip(outs, splits):
        for c0 in range(0, n, 512):
            cw = min(512, n - c0)
            o_ref[:, c0:c0 + cw] = _dot(xm, w_ref[:, off + c0:off + c0 + cw]).astype(o_ref.dtype)
        off += n
    if with_t:
        outs[len(splits)][...] = _dot_nt(wt_ref[...], xm)


def fused_proj(lay, x, shift, scale, w, splits, wt=None):
    ntot = sum(n for n, _ in splits)
    assert w.shape == (D_MODEL, ntot)
    mod_spec = pl.BlockSpec((1, 1, D_MODEL), lambda i: (lay.mod_row(i), 0, 0))
    in_specs = [pl.BlockSpec((TM, D_MODEL), lambda i: (i, 0)), mod_spec, mod_spec,
                pl.BlockSpec((D_MODEL, ntot), lambda i: (0, 0))]
    args = [x, shift, scale, w]
    out_specs = [pl.BlockSpec((TM, n), lambda i: (i, 0)) for n, _ in splits]
    out_shape = [jax.ShapeDtypeStruct((lay.t, n), dt) for n, dt in splits]
    if wt is not None:
        r = wt.shape[0]
        in_specs.append(pl.BlockSpec((r, D_MODEL), lambda i: (0, 0)))
        args.append(wt)
        out_specs.append(pl.BlockSpec((r, TM), lambda i: (0, i)))
        out_shape.append(jax.ShapeDtypeStruct((r, lay.t), F32))
    return pl.pallas_call(
        functools.partial(_proj_kernel, splits=tuple(splits), with_t=wt is not None),
        grid=(lay.nt,),
        in_specs=in_specs, out_specs=out_specs, out_shape=out_shape,
        compiler_params=_cp("parallel"),
        name="fused_proj",
    )(*args)


def _outproj_kernel(*refs, n_mix, alpha):
    x_ref = refs[0]
    mix = refs[1:1 + n_mix]
    ws = refs[1 + n_mix:1 + 2 * n_mix]
    gate_ref, g_ref, b_ref, sh_ref, sc_ref, wr_hi, wr_lo, br_ref = refs[1 + 2 * n_mix:9 + 2 * n_mix]
    xo_ref, h_ref, lg_ref = refs[9 + 2 * n_mix:]
    y = _dot(mix[0][...], ws[0][...])
    for a, w in zip(mix[1:], ws[1:]):
        y = y + _dot(a[...], w[...])
    xn = _layer_norm(alpha * x_ref[...] + gate_ref[0] * y, g_ref[...], b_ref[...])
    xo_ref[...] = xn
    h = xn * (1.0 + sc_ref[0]) + sh_ref[0]
    h_ref[...] = h
    h_hi = h.astype(BF16)
    h_lo = (h - h_hi.astype(F32)).astype(BF16)
    lg_ref[...] = _dot(h_hi, wr_hi[...]) + (_dot(h_lo, wr_hi[...]) + _dot(h_hi, wr_lo[...])) + br_ref[...]


def outproj_ln(lay, x, mixes, ws, gate, ln_g, ln_b, shift, scale, wr_hi, wr_lo, br, alpha):
    mod_spec = pl.BlockSpec((1, 1, D_MODEL), lambda i: (lay.mod_row(i), 0, 0))
    row_spec = pl.BlockSpec((1, D_MODEL), lambda i: (0, 0))
    in_specs = [pl.BlockSpec((TM, D_MODEL), lambda i: (i, 0))]
    in_specs += [pl.BlockSpec((TM, a.shape[1]), lambda i: (i, 0)) for a in mixes]
    in_specs += [pl.BlockSpec(w.shape, lambda i: (0, 0)) for w in ws]
    in_specs += [mod_spec, row_spec, row_spec, mod_spec, mod_spec,
                 pl.BlockSpec((D_MODEL, LANE), lambda i: (0, 0)), pl.BlockSpec((D_MODEL, LANE), lambda i: (0, 0)),
                 pl.BlockSpec((1, LANE), lambda i: (0, 0))]
    return pl.pallas_call(
        functools.partial(_outproj_kernel, n_mix=len(mixes), alpha=alpha),
        grid=(lay.nt,),
        in_specs=in_specs,
        out_specs=[pl.BlockSpec((TM, D_MODEL), lambda i: (i, 0)), pl.BlockSpec((TM, D_MODEL), lambda i: (i, 0)),
                   pl.BlockSpec((TM, LANE), lambda i: (i, 0))],
        out_shape=[jax.ShapeDtypeStruct((lay.t, D_MODEL), F32), jax.ShapeDtypeStruct((lay.t, D_MODEL), F32),
                   jax.ShapeDtypeStruct((lay.t, LANE), F32)],
        compiler_params=_cp("parallel"),
        name="outproj_ln",
    )(x, *mixes, *ws, gate, ln_g, ln_b, shift, scale, wr_hi, wr_lo, br)


def _na_kernel(q_ref, kl_ref, vl_ref, kc_ref, vc_ref, bt_ref, o_ref, *, ncq, rows):
    j = pl.program_id(1)
    q = q_ref[...]
    kc = kc_ref[...]
    vc = vc_ref[...]
    lane = lax.broadcasted_iota(jnp.int32, (1, NA_DIM), 1)
    scale = NA_HEAD_DIM ** -0.5

    def heads(local):
        masks = [(lane >= h * NA_HEAD_DIM) & (lane < (h + 1) * NA_HEAD_DIM) for h in range(NA_HEADS)]
        qs = jnp.concatenate([jnp.where(hm, q, jnp.zeros_like(q)) for hm in masks], axis=0)
        s_c = _dot_nt(qs, kc) * scale
        m = jnp.max(s_c, -1, keepdims=True)
        if local:
            r = j - ncq
            r0 = jnp.clip(r - NA_ROWS // 2, 0, rows - NA_ROWS)
            dr0 = r0 - r + (NA_ROWS - 1)
            start = pl.multiple_of(r0 * GRID_W, GRID_W)
            kb = kl_ref[pl.ds(start, NA_ROWS * GRID_W), :]
            vb = vl_ref[pl.ds(start, NA_ROWS * GRID_W), :]
            s_l = _dot_nt(qs, kb) * scale + bt_ref[dr0].reshape(NA_HEADS * GRID_W, NA_ROWS * GRID_W)
            m = jnp.maximum(m, jnp.max(s_l, -1, keepdims=True))
            p_l = jnp.exp(s_l - m)
        p_c = jnp.exp(s_c - m)
        den = jnp.sum(p_c, -1, keepdims=True)
        o = _dot(p_c.astype(BF16), vc)
        if local:
            den = den + jnp.sum(p_l, -1, keepdims=True)
            o = o + _dot(p_l.astype(BF16), vb)
        o = o / den
        acc = jnp.zeros((GRID_W, NA_DIM), F32)
        for h, hm in enumerate(masks):
            acc = jnp.where(hm, o[h * GRID_W:(h + 1) * GRID_W], acc)
        o_ref[...] = acc.astype(o_ref.dtype)

    @pl.when(j < ncq)
    def _():
        heads(False)

    @pl.when(j >= ncq)
    def _():
        heads(True)


def na_attention(lay, qkv, bias_tab):
    b, c, l = lay.b, lay.c, lay.l
    ncq = c // GRID_W
    rows = lay.rows
    assert rows >= NA_ROWS
    qb = lambda bi, j: jnp.where(j < ncq, bi * ncq + j, lay.nc // GRID_W + bi * rows + (j - ncq))
    return pl.pallas_call(
        functools.partial(_na_kernel, ncq=ncq, rows=rows),
        grid=(b, ncq + rows),
        in_specs=[pl.BlockSpec((GRID_W, NA_DIM), lambda bi, j: (qb(bi, j), 0)),
                  pl.BlockSpec((l, NA_DIM), lambda bi, j: (lay.nc // l + bi, 1)),
                  pl.BlockSpec((l, NA_DIM), lambda bi, j: (lay.nc // l + bi, 2)),
                  pl.BlockSpec((c, NA_DIM), lambda bi, j: (bi, 1)),
                  pl.BlockSpec((c, NA_DIM), lambda bi, j: (bi, 2)),
                  pl.BlockSpec(bias_tab.shape, lambda bi, j: (0, 0, 0, 0))],
        out_specs=pl.BlockSpec((GRID_W, NA_DIM), lambda bi, j: (qb(bi, j), 0)),
        out_shape=jax.ShapeDtypeStruct((lay.t, NA_DIM), BF16),
        compiler_params=_cp("parallel", "arbitrary"),
        name="na_attention",
    )(qkv, qkv, qkv, qkv, qkv, bias_tab)


def na_bias_table(rpb):
    kw = NA_COLS
    col_start = np.clip(np.arange(GRID_W) - kw // 2, 0, GRID_W - kw)
    kc = np.arange(GRID_W)[None, :]
    jq = np.arange(GRID_W)[:, None]
    inside = (kc >= col_start[:, None]) & (kc < col_start[:, None] + kw)
    dc = np.clip(kc - jq + (NA_COLS - 1), 0, 2 * NA_COLS - 2)
    planes = jnp.where(inside[None, None], rpb[:, :, dc], NEG)
    tabs = []
    for dr0 in range(NA_ROWS):
        sel = planes[:, dr0:dr0 + NA_ROWS]
        tabs.append(jnp.transpose(sel, (0, 2, 1, 3)).reshape(NA_HEADS, GRID_W, NA_ROWS * GRID_W))
    return jnp.stack(tabs, 0).astype(F32)


def _conv_kernel(x_ref, p_ref, n_ref, w_ref, b_ref, o_ref, *, nct, tpc, tpl):
    i = pl.program_id(0)
    first = jnp.where(i < nct, i % tpc == 0, (i - nct) % tpl == 0)
    last = jnp.where(i < nct, i % tpc == tpc - 1, (i - nct) % tpl == tpl - 1)
    half = SSD_CONV // 2
    for c0 in range(0, XBC_DIM, 512):
        cs = slice(c0, c0 + 512)
        prev = jnp.where(first, 0.0, p_ref[:, cs])
        nxt = jnp.where(last, 0.0, n_ref[:, cs])
        ext = jnp.concatenate([prev, x_ref[:, cs], nxt], axis=0)
        n_ext = TM + 16
        acc = jnp.zeros((TM, 512), F32) + b_ref[:, cs]
        for k in range(SSD_CONV):
            shifted = ext if k == half else pltpu.roll(ext, (half - k) % n_ext, 0)
            acc = acc + w_ref[k:k + 1, cs] * shifted[8:8 + TM]
        o_ref[:, cs] = _silu(acc)


def conv_silu(lay, xbc, w8, bias):
    n8 = lay.t // 8
    return pl.pallas_call(
        functools.partial(_conv_kernel, nct=lay.nct, tpc=lay.c // TM, tpl=lay.tpb),
        grid=(lay.nt,),
        in_specs=[pl.BlockSpec((TM, XBC_DIM), lambda i: (i, 0)),
                  pl.BlockSpec((8, XBC_DIM), lambda i: (jnp.maximum(i * (TM // 8) - 1, 0), 0)),
                  pl.BlockSpec((8, XBC_DIM), lambda i: (jnp.minimum((i + 1) * (TM // 8), n8 - 1), 0)),
                  pl.BlockSpec((8, XBC_DIM), lambda i: (0, 0)),
                  pl.BlockSpec((1, XBC_DIM), lambda i: (0, 0))],
        out_specs=pl.BlockSpec((TM, XBC_DIM), lambda i: (i, 0)),
        out_shape=jax.ShapeDtypeStruct((lay.t, XBC_DIM), F32),
        compiler_params=_cp("parallel"),
        name="conv_silu",
    )(xbc, xbc, xbc, w8, bias)


def _ssd_kernel(*refs, rev):
    if rev:
        (u_ref, dt_ref, dtt_ref, bn_ref, an_ref, bc_ref, ac_ref, e_ref,
         yf_ref, z_ref, dsk_ref, gn_ref, o_ref, state, ybuf) = refs
    else:
        u_ref, dt_ref, dtt_ref, bn_ref, an_ref, bc_ref, ac_ref, e_ref, o_ref, state, ybuf = refs
    qn = SSD_CHUNK

    @pl.when(pl.program_id(1) == 0)
    def _():
        state[...] = jnp.zeros_like(state)

    xs = u_ref[:, :SSD_INNER]
    dt = _softplus(dt_ref[...] + bn_ref[...])
    a = dt * an_ref[...]
    dt_t = _softplus(dtt_ref[...] + bc_ref[...])
    a_t = dt_t * ac_ref[...]
    li = lax.broadcasted_iota(jnp.int32, (qn, qn), 0)
    si = lax.broadcasted_iota(jnp.int32, (qn, qn), 1)
    keep = (si >= li) if rev else (si <= li)
    tri = keep.astype(BF16)
    tri_t = ((li >= si) if rev else (li <= si)).astype(BF16)
    a_cs = _dot3_r(tri, a)
    a_cs_t = _dot3_l(a_t, tri_t)
    e = e_ref[...]
    dt_x = _dot3_l(dt, e)
    acs_x = _dot3_l(a_cs, e)
    end = 0 if rev else qn - 1
    acs_end = acs_x[end:end + 1, :]
    xdt = xs * dt_x
    xdt_b = xdt.astype(BF16)
    xdte = (xdt * jnp.exp(acs_end - acs_x)).astype(BF16)
    ea_x = jnp.exp(acs_x)
    cd_x = jnp.exp(acs_end)
    lane = lax.broadcasted_iota(jnp.int32, (1, LANE), 1)
    lo = lane < SSD_HEAD_DIM
    cb = None
    for p in range(SSD_HEADS // 2):
        g = p // 2
        ps = slice(p * LANE, (p + 1) * LANE)
        if p % 2 == 0:
            bg = u_ref[:, SSD_INNER + g * SSD_STATE:SSD_INNER + (g + 1) * SSD_STATE]
            cg = u_ref[:, SSD_INNER + SSD_BC + g * SSD_STATE:SSD_INNER + SSD_BC + (g + 1) * SSD_STATE].astype(BF16)
            cb = _dot_nt(cg, bg.astype(BF16))
            bg_t = bg.T.astype(BF16)
        xp = xdt_b[:, ps]
        y = None
        for hh in range(2):
            h = 2 * p + hh
            seg = a_cs[:, h:h + 1] - a_cs_t[h:h + 1, :]
            m = (jnp.exp(jnp.where(keep, seg, NEG)) * cb).astype(BF16)
            xh = jnp.where(lo if hh == 0 else jnp.logical_not(lo), xp, jnp.zeros_like(xp))
            yh = _dot(m, xh)
            y = yh if y is None else y + yh
        st = state[p]
        y = y + _dot(cg, st.astype(BF16)) * ea_x[:, ps]
        state[p] = cd_x[:, ps] * st + _dot(bg_t, xdte[:, ps])
        ybuf[:, ps] = y
    if rev:
        y = dsk_ref[...] * xs + yf_ref[...] + ybuf[...]
        y = y * _silu(z_ref[...])
        o_ref[...] = (y * lax.rsqrt(jnp.mean(y * y, -1, keepdims=True) + NORM_EPS) * gn_ref[...]).astype(o_ref.dtype)
    else:
        o_ref[...] = ybuf[...]


def ssd_pass(lay, rev, u, dt, dtt, bias_nat, a_nat, bias_col, a_col, expand, extra=None):
    qn = SSD_CHUNK
    ncc, ncl = lay.c // qn, lay.l // qn
    nctq = lay.nc // qn
    d = 1 if rev else 0

    def chunk(bi, j):
        jc = (ncc - 1 - j) if rev else j
        jl = (ncl - 1 - (j - ncc)) if rev else (j - ncc)
        return jnp.where(j < ncc, bi * ncc + jc, nctq + bi * ncl + jl)

    row = lambda n: pl.BlockSpec((1, n), lambda bi, j: (0, 0))
    in_specs = [pl.BlockSpec((qn, XBC_DIM), lambda bi, j: (chunk(bi, j), 0)),
                pl.BlockSpec((qn, LANE), lambda bi, j: (chunk(bi, j), d)),
                pl.BlockSpec((SSD_HEADS, qn), lambda bi, j: (d, chunk(bi, j))),
                row(LANE), row(LANE),
                pl.BlockSpec((SSD_HEADS, 1), lambda bi, j: (0, 0)), pl.BlockSpec((SSD_HEADS, 1), lambda bi, j: (0, 0)),
                pl.BlockSpec((LANE, SSD_INNER), lambda bi, j: (0, 0))]
    args = [u, dt, dtt, bias_nat, a_nat, bias_col, a_col, expand]
    if rev:
        yf, z, dsk, gn = extra
        in_specs += [pl.BlockSpec((qn, SSD_INNER), lambda bi, j: (chunk(bi, j), 0)),
                     pl.BlockSpec((qn, SSD_INNER), lambda bi, j: (chunk(bi, j), 0)),
                     row(SSD_INNER), row(SSD_INNER)]
        args += [yf, z, dsk, gn]
    return pl.pallas_call(
        functools.partial(_ssd_kernel, rev=rev),
        grid=(lay.b, ncc + ncl),
        in_specs=in_specs,
        out_specs=pl.BlockSpec((qn, SSD_INNER), lambda bi, j: (chunk(bi, j), 0)),
        out_shape=jax.ShapeDtypeStruct((lay.t, SSD_INNER), BF16 if rev else F32),
        scratch_shapes=[pltpu.VMEM((SSD_HEADS // 2, SSD_STATE, LANE), F32), pltpu.VMEM((qn, SSD_INNER), F32)],
        compiler_params=_cp("parallel", "arbitrary"),
        name="ssd_bwd" if rev else "ssd_fwd",
    )(*args)


def _rope_tables(lay):
    t = np.arange(lay.l)
    pos_r, pos_c = t // GRID_W, t % GRID_W

    def half_tables(half):
        inv = ROPE_THETA ** (-np.arange(0, half, 2, dtype=np.float32) / half)
        cos, sin = [], []
        for pos in (pos_r, pos_c):
            ang = pos.astype(np.float32)[:, None] * inv[None, :]
            cos += [np.cos(ang), np.cos(ang)]
            sin += [-np.sin(ang), np.sin(ang)]
        return np.concatenate(cos, 1), np.concatenate(sin, 1)

    c32, s32 = half_tables(MLA_ROPE // 2)
    c64, s64 = half_tables(GQA_HEAD_DIM // 2)
    mla_c = np.concatenate([np.ones((lay.l, MLA_NOPE)), c32, np.zeros((lay.l, LANE - MLA_NOPE - MLA_ROPE))], 1)
    mla_s = np.concatenate([np.zeros((lay.l, MLA_NOPE)), s32, np.zeros((lay.l, LANE - MLA_NOPE - MLA_ROPE))], 1)
    ident_c = np.concatenate([np.ones((TM, MLA_NOPE + MLA_ROPE)), np.zeros((TM, LANE - MLA_NOPE - MLA_ROPE))], 1)
    gqa_c = np.concatenate([c64, c64], 1)
    gqa_s = np.concatenate([s64, s64], 1)
    f = lambda ident, tab: jnp.asarray(np.concatenate([ident, tab], 0), F32)
    return (f(ident_c, mla_c), f(np.zeros((TM, LANE)), mla_s),
            f(np.ones((TM, LANE)), gqa_c), f(np.zeros((TM, LANE)), gqa_s))


def _prep_kernel(cq_ref, ckv_ref, gq_ref, gqp_ref, gk_ref, gkp_ref, kra_ref, krb_ref, gv_ref,
                 qn_ref, kvn_ref, gqn_ref, gqnp_ref, gkn_ref, gknp_ref,
                 wq_ref, wqp_ref, wk_ref, wv_ref, bd_ref, mc_ref, ms_ref, gc_ref, gs_ref,
                 mq_ref, mk_ref, mv_ref, oq_ref, ok_ref, ov_ref):
    def rms(x, g):
        return x * lax.rsqrt(jnp.mean(x * x, -1, keepdims=True) + NORM_EPS) * g

    mc, ms = mc_ref[...], ms_ref[...]
    mc8 = jnp.concatenate([mc] * MLA_HEADS, 1)
    ms8 = jnp.concatenate([ms] * MLA_HEADS, 1)
    rq = rms(cq_ref[...], qn_ref[...]).astype(BF16)
    q = _dot(rq, wq_ref[...]) * mc8 + _dot(rq, wqp_ref[...]) * ms8
    mq_ref[...] = (q * MLA_SCALE).astype(BF16)
    rkv = rms(ckv_ref[...], kvn_ref[...]).astype(BF16)
    kr = kra_ref[...] * mc + krb_ref[...] * ms
    mk_ref[...] = (_dot(rkv, wk_ref[...]) + jnp.concatenate([kr] * MLA_HEADS, 1)).astype(BF16)
    one_lane = lax.broadcasted_iota(jnp.int32, (1, D_MODEL), 1) % LANE == V_ONE_LANE
    mv_ref[...] = jnp.where(one_lane, 1.0, _dot(rkv, wv_ref[...])).astype(BF16)
    ov_ref[...] = jnp.where(one_lane[:, :2 * LANE], jnp.ones((), BF16), gv_ref[...])
    gc4 = jnp.concatenate([gc_ref[...]] * 4, 1)
    gs4 = jnp.concatenate([gs_ref[...]] * 4, 1)
    bd = bd_ref[...]

    def qk_norm_rope(u, up, g, gp, scale):
        sq = u * u
        hi = sq.astype(BF16)
        lo = (sq - hi.astype(F32)).astype(BF16)
        r = lax.rsqrt(_dot(hi, bd) + _dot(lo, bd) + NORM_EPS)
        return ((u * g * gc4 + up * gp * gs4) * (r * scale)).astype(BF16)

    oq_ref[...] = qk_norm_rope(gq_ref[...], gqp_ref[...], gqn_ref[...], gqnp_ref[...], GQA_SCALE)
    ok_ref[...] = qk_norm_rope(gk_ref[...], gkp_ref[...], gkn_ref[...], gknp_ref[...], 1.0)


def odd_prep(lay, acts, norms, weights, tables):
    tab_row = lambda i: jnp.where(i < lay.nct, 0, 1 + (i - lay.nct) % lay.tpb)
    tok = lambda n: pl.BlockSpec((TM, n), lambda i: (i, 0))
    const = lambda a: pl.BlockSpec(a.shape, lambda i: (0, 0))
    in_specs = [tok(a.shape[1]) for a in acts] + [const(a) for a in norms] + [const(a) for a in weights]
    in_specs += [pl.BlockSpec((TM, LANE), lambda i: (tab_row(i), 0)) for _ in tables]
    widths = (D_MODEL, D_MODEL, D_MODEL, 512, 512, 2 * LANE)
    return pl.pallas_call(
        _prep_kernel,
        grid=(lay.nt,),
        in_specs=in_specs,
        out_specs=[tok(n) for n in widths],
        out_shape=[jax.ShapeDtypeStruct((lay.t, n), BF16) for n in widths],
        compiler_params=_cp("parallel"),
        name="odd_prep",
    )(*acts, *norms, *weights, *tables)


def _lane_block_max(s):
    m = s[:, :LANE]
    for c0 in range(LANE, s.shape[1], LANE):
        m = jnp.maximum(m, s[:, c0:c0 + LANE])
    return m


def _attn_kernel(q_ref, kc_ref, kl_ref, vc_ref, vl_ref, o_ref, *, ncq, q_shared, v_shared):
    j = pl.program_id(2)
    lane = lax.broadcasted_iota(jnp.int32, (1, LANE), 1)

    def run(with_lat):
        outs = []
        for hh in range(2):
            q = q_ref[...] if q_shared else q_ref[:, hh * LANE:(hh + 1) * LANE]
            hs = slice(hh * LANE, (hh + 1) * LANE)
            vs = slice(0, LANE) if v_shared else hs
            s_c = _dot_nt(q, kc_ref[:, hs]).astype(BF16)
            m = _lane_block_max(s_c)
            if with_lat:
                s_l = _dot_nt(q, kl_ref[:, hs]).astype(BF16)
                m = jnp.maximum(m, _lane_block_max(s_l))
            m = jnp.max(m.astype(F32), -1, keepdims=True).astype(BF16)
            o = _dot(jnp.exp(s_c - m), vc_ref[:, vs])
            if with_lat:
                o = o + _dot(jnp.exp(s_l - m), vl_ref[:, vs])
            outs.append(o / o[:, V_ONE_LANE:V_ONE_LANE + 1])
        out = jnp.where(lane < V_ONE_LANE, outs[0], pltpu.roll(outs[1], V_ONE_LANE, 1))
        o_ref[...] = out.astype(o_ref.dtype)

    @pl.when(j < ncq)
    def _():
        run(False)

    @pl.when(j >= ncq)
    def _():
        run(True)


def dense_attention(lay, q, k, v, shared, kv_block):
    b, c, l = lay.b, lay.c, lay.l
    tq = ATT_TQ
    ncq, nlq = c // tq, l // tq
    qw = LANE if shared else 2 * LANE
    qb = lambda bi, j: jnp.where(j < ncq, bi * ncq + j, lay.nc // tq + bi * nlq + (j - ncq))
    ctx_spec = pl.BlockSpec((c, 2 * LANE), lambda bi, p, j: (bi, kv_block(p)))
    lat_spec = pl.BlockSpec((l, 2 * LANE), lambda bi, p, j: (lay.nc // l + bi, kv_block(p)))
    ctx_v = pl.BlockSpec((c, qw), lambda bi, p, j: (bi, kv_block(p)))
    lat_v = pl.BlockSpec((l, qw), lambda bi, p, j: (lay.nc // l + bi, kv_block(p)))
    return pl.pallas_call(
        functools.partial(_attn_kernel, ncq=ncq, q_shared=shared, v_shared=shared),
        grid=(b, 4, ncq + nlq),
        in_specs=[pl.BlockSpec((tq, qw), lambda bi, p, j: (qb(bi, j), p)), ctx_spec, lat_spec, ctx_v, lat_v],
        out_specs=pl.BlockSpec((tq, LANE), lambda bi, p, j: (qb(bi, j), p)),
        out_shape=jax.ShapeDtypeStruct((lay.t, 4 * LANE), BF16),
        compiler_params=_cp("parallel", "parallel", "arbitrary"),
        name="dense_attention",
    )(q, k, k, v, v)


GATHER_UNROLL = 8


def _moe_ffn_kernel(be_ref, nb_ref, tok_hbm, h_hbm, sw_ref, w1_ref, w3_ref, w2_ref, y_ref, idx, xbuf, wb1, wb3, wb2,
                    sem_i, sem_g):
    i = pl.program_id(0)
    n_used = nb_ref[0]

    def idx_copy(blk):
        slot = blk % 2
        return pltpu.make_async_copy(tok_hbm.at[pl.ds(blk * MOE_TM, MOE_TM)], idx.at[slot], sem_i.at[slot])

    def gather(blk):
        slot = blk % 2

        def issue(r, carry):
            pltpu.make_async_copy(h_hbm.at[pl.ds(idx[slot, r], 1), :], xbuf.at[slot, pl.ds(r, 1), :],
                                  sem_g.at[slot]).start()
            return carry

        lax.fori_loop(0, MOE_TM, issue, 0, unroll=GATHER_UNROLL)

    @pl.when((i == 0) & (n_used > 0))
    def _():
        idx_copy(0).start()
        idx_copy(0).wait()
        gather(0)

        @pl.when(n_used > 1)
        def _():
            idx_copy(1).start()

    @pl.when(i + 1 < n_used)
    def _():
        idx_copy(i + 1).wait()
        gather(i + 1)

    @pl.when(i + 2 < n_used)
    def _():
        idx_copy(i + 2).start()

    @pl.when(i < n_used)
    def _():
        slot = i % 2
        pltpu.make_async_copy(h_hbm.at[pl.ds(0, MOE_TM), :], xbuf.at[slot], sem_g.at[slot]).wait()

        @pl.when((i == 0) | (be_ref[i] != be_ref[jnp.maximum(i - 1, 0)]))
        def _():
            wb1[...] = w1_ref[0, 0].astype(BF16)
            wb3[...] = w3_ref[0, 0].astype(BF16)
            wb2[...] = w2_ref[0, 0].astype(BF16)

        x = xbuf[slot].astype(BF16)
        hid = (_silu(_dot(x, wb1[...])) * _dot(x, wb3[...])).astype(BF16)
        y_ref[...] = _dot(hid, wb2[...]) * sw_ref[...]

    @pl.when(i >= n_used)
    def _():
        y_ref[...] = jnp.zeros_like(y_ref)


def moe_ffn(h, blk_e, n_used, slot_tok, slot_w, w1, w3, w2, layer):
    nb = blk_e.shape[0]
    wspec = lambda shape: pl.BlockSpec((1, 1) + shape, lambda i, be, nu: (layer, be[i], 0, 0))
    return pl.pallas_call(
        _moe_ffn_kernel,
        grid_spec=pltpu.PrefetchScalarGridSpec(
            num_scalar_prefetch=2,
            grid=(nb,),
            in_specs=[pl.BlockSpec(memory_space=pl.ANY), pl.BlockSpec(memory_space=pl.ANY),
                      pl.BlockSpec((MOE_TM, 1), lambda i, be, nu: (i, 0)),
                      wspec((D_MODEL, MOE_HIDDEN)), wspec((D_MODEL, MOE_HIDDEN)), wspec((MOE_HIDDEN, D_MODEL))],
            out_specs=pl.BlockSpec((MOE_TM, D_MODEL), lambda i, be, nu: (i, 0)),
            scratch_shapes=[pltpu.SMEM((2, MOE_TM), jnp.int32), pltpu.VMEM((2, MOE_TM, D_MODEL), F32),
                            pltpu.VMEM((D_MODEL, MOE_HIDDEN), BF16), pltpu.VMEM((D_MODEL, MOE_HIDDEN), BF16),
                            pltpu.VMEM((MOE_HIDDEN, D_MODEL), BF16),
                            pltpu.SemaphoreType.DMA((2,)), pltpu.SemaphoreType.DMA((2,))]),
        out_shape=jax.ShapeDtypeStruct((nb * MOE_TM, D_MODEL), F32),
        compiler_params=_cp("arbitrary"),
        name="moe_ffn",
    )(blk_e, n_used, slot_tok, h, slot_w, w1, w3, w2)


def _combine_kernel(pos_hbm, y_hbm, x_ref, gate_ref, g_ref, b_ref, o_ref, idx, buf, sem_i, sem_g, *, alpha):
    i = pl.program_id(0)
    n = pl.num_programs(0)

    def idx_copy(tile):
        slot = tile % 2
        return pltpu.make_async_copy(pos_hbm.at[pl.ds(tile * (MOE_TOPK * TM), MOE_TOPK * TM)], idx.at[slot],
                                     sem_i.at[slot])

    def gather(tile):
        slot = tile % 2

        def issue(r, carry):
            for k in range(MOE_TOPK):
                pltpu.make_async_copy(y_hbm.at[pl.ds(idx[slot, MOE_TOPK * r + k], 1), :],
                                      buf.at[slot, k, pl.ds(r, 1), :], sem_g.at[slot]).start()
            return carry

        lax.fori_loop(0, TM, issue, 0, unroll=GATHER_UNROLL // MOE_TOPK)

    @pl.when(i == 0)
    def _():
        idx_copy(0).start()
        idx_copy(0).wait()
        gather(0)

        @pl.when(n > 1)
        def _():
            idx_copy(1).start()

    @pl.when(i + 1 < n)
    def _():
        idx_copy(i + 1).wait()
        gather(i + 1)

    @pl.when(i + 2 < n)
    def _():
        idx_copy(i + 2).start()

    slot = i % 2
    for k in range(MOE_TOPK):
        pltpu.make_async_copy(y_hbm.at[pl.ds(0, TM), :], buf.at[slot, k], sem_g.at[slot]).wait()
    f = buf[slot, 0] + buf[slot, 1]
    o_ref[...] = _layer_norm(alpha * x_ref[...] + gate_ref[0] * f, g_ref[...], b_ref[...])


def moe_combine_ln(lay, pos, y, x, gate, ln_g, ln_b, alpha):
    row_spec = pl.BlockSpec((1, D_MODEL), lambda i: (0, 0))
    return pl.pallas_call(
        functools.partial(_combine_kernel, alpha=alpha),
        grid=(lay.nt,),
        in_specs=[pl.BlockSpec(memory_space=pl.ANY), pl.BlockSpec(memory_space=pl.ANY),
                  pl.BlockSpec((TM, D_MODEL), lambda i: (i, 0)),
                  pl.BlockSpec((1, 1, D_MODEL), lambda i: (lay.mod_row(i), 0, 0)), row_spec, row_spec],
        out_specs=pl.BlockSpec((TM, D_MODEL), lambda i: (i, 0)),
        out_shape=jax.ShapeDtypeStruct((lay.t, D_MODEL), F32),
        scratch_shapes=[pltpu.SMEM((2, MOE_TOPK * TM), jnp.int32), pltpu.VMEM((2, MOE_TOPK, TM, D_MODEL), F32),
                        pltpu.SemaphoreType.DMA((2,)), pltpu.SemaphoreType.DMA((2,))],
        compiler_params=_cp("arbitrary"),
        name="moe_combine_ln",
    )(pos, y, x, gate, ln_g, ln_b)


def moe_route(logits):
    t = logits.shape[0]
    g_logit = logits[:, :MOE_GROUPS]
    g_sel = jnp.argmax(g_logit, axis=-1).astype(jnp.int32)
    g_gate = 1.0 / jnp.sum(jnp.exp(g_logit - jnp.max(g_logit, -1, keepdims=True)), -1)
    e_logits = logits[:, MOE_GROUPS:MOE_GROUPS + MOE_EXPERTS].reshape(t, MOE_GROUPS, MOE_EPG)
    g_hot = g_sel[:, None] == jnp.arange(MOE_GROUPS, dtype=jnp.int32)[None, :]
    e_in = jnp.sum(jnp.where(g_hot[:, :, None], e_logits, 0.0), axis=1)
    lanes = jnp.arange(MOE_EPG, dtype=jnp.int32)[None, :]
    i1 = jnp.argmax(e_in, axis=-1).astype(jnp.int32)
    v1 = jnp.max(e_in, axis=-1)
    rest = jnp.where(lanes == i1[:, None], -jnp.inf, e_in)
    i2 = jnp.argmax(rest, axis=-1).astype(jnp.int32)
    v2 = jnp.max(rest, axis=-1)
    p2 = jnp.exp(v2 - v1)
    gate = g_gate[:, None] * jnp.stack([1.0 / (1.0 + p2), p2 / (1.0 + p2)], -1)
    expert = g_sel[:, None] * MOE_EPG + jnp.stack([i1, i2], -1)
    return expert.astype(jnp.int32), gate


def moe_dispatch_plan(expert, gate):
    t = expert.shape[0]
    n_assign = t * MOE_TOPK
    flat_e = expert.reshape(n_assign)
    flat_w = gate.reshape(n_assign)
    eids = jnp.arange(MOE_EXPERTS, dtype=jnp.int32)
    order = jnp.argsort(flat_e).astype(jnp.int32)
    rank = jnp.argsort(order).astype(jnp.int32)
    hot = flat_e[:, None] == eids[None, :]
    counts = jnp.sum(hot, axis=0, dtype=jnp.int32)
    starts = jnp.cumsum(counts) - counts
    pcounts = (counts + MOE_TM - 1) // MOE_TM * MOE_TM
    pends = jnp.cumsum(pcounts)
    pstarts = pends - pcounts
    nb = (n_assign + MOE_EXPERTS * (MOE_TM - 1)) // MOE_TM
    blk_start = jnp.arange(nb, dtype=jnp.int32) * MOE_TM
    blk_e = jnp.minimum(jnp.sum(pends[None, :] <= blk_start[:, None], axis=1), MOE_EXPERTS - 1).astype(jnp.int32)
    n_used = (pends[-1] // MOE_TM).astype(jnp.int32).reshape(1)
    blk_hot = blk_e[:, None] == eids[None, :]
    pick = lambda table: jnp.sum(jnp.where(blk_hot, table[None, :], 0), axis=1)[:, None]
    in_blk = jnp.arange(MOE_TM, dtype=jnp.int32)[None, :]
    seg_rank = blk_start[:, None] + in_blk - pick(pstarts)
    valid = (seg_rank < pick(counts)).reshape(-1)
    src = jnp.clip(pick(starts) + seg_rank, 0, n_assign - 1)
    assign = order[src.reshape(-1)]
    slot_tok = jnp.where(valid, assign // MOE_TOPK, 0).astype(jnp.int32)
    slot_w = jnp.where(valid, flat_w[assign], 0.0).astype(F32)
    pos = (rank + jnp.sum(jnp.where(hot, (pstarts - starts)[None, :], 0), axis=1)).astype(jnp.int32)
    return blk_e, n_used, slot_tok, slot_w.reshape(-1, 1), pos


def _rope_partner(dim):
    q = dim // 4
    idx = np.arange(dim)
    return np.where((idx // q) % 2 == 0, idx + q, idx - q)


def _even_weights(w_in, w_out, conv_w, conv_b, dt_bias, a_log, d_skip, gnorm_w):
    n_main = NA_DIM * 3 + SSD_INNER + XBC_DIM
    w_dt = jnp.zeros((D_MODEL, 2 * LANE), F32)
    w_dt = w_dt.at[:, :SSD_HEADS].set(w_in[:, n_main:n_main + SSD_HEADS])
    w_dt = w_dt.at[:, LANE:LANE + SSD_HEADS].set(w_in[:, n_main + SSD_HEADS:])
    w = jnp.concatenate([w_in[:, :n_main], w_dt], 1).astype(BF16)
    wt = w_in[:, n_main:].T.astype(BF16)
    pad = lambda v: jnp.zeros((1, LANE), F32).at[0, :SSD_HEADS].set(v)
    a = -jnp.exp(a_log.astype(F32))
    expand = np.zeros((LANE, SSD_INNER), np.float32)
    for h in range(SSD_HEADS):
        expand[h, h * SSD_HEAD_DIM:(h + 1) * SSD_HEAD_DIM] = 1.0
    return dict(
        w=w, wt=wt,
        conv_w=jnp.zeros((8, XBC_DIM), F32).at[:SSD_CONV].set(conv_w), conv_b=conv_b.reshape(1, XBC_DIM),
        bias_nat=[pad(dt_bias[d]) for d in range(2)], a_nat=[pad(a[d]) for d in range(2)],
        bias_col=[dt_bias[d].reshape(SSD_HEADS, 1) for d in range(2)], a_col=[a[d].reshape(SSD_HEADS, 1) for d in range(2)],
        expand=jnp.asarray(expand, BF16),
        dskip=jnp.repeat(d_skip, SSD_HEAD_DIM).reshape(1, SSD_INNER), gnorm=gnorm_w.reshape(1, SSD_INNER),
        w_out=[w_out[:NA_DIM].astype(BF16), w_out[NA_DIM:].astype(BF16)])


def _odd_weights(w_in, w_out, q_norm, kv_norm, w_uq, w_ukv, gq_norm, gk_norm):
    o = np.cumsum([0, MLA_Q_LORA, GQA_HEADS * GQA_HEAD_DIM, MLA_KV_LORA, MLA_ROPE, 128, 128])
    w_cq, w_gq, w_ckv, w_kr, w_gk, w_gv = [w_in[:, o[i]:o[i + 1]] for i in range(6)]
    p64, p32 = _rope_partner(GQA_HEAD_DIM), _rope_partner(MLA_ROPE)
    head_perm = lambda n: np.concatenate([h * GQA_HEAD_DIM + p64 for h in range(n)])
    z64 = jnp.zeros((D_MODEL, GQA_HEAD_DIM), F32)

    def placed(wm):
        out = []
        for g in range(GQA_KV_HEADS):
            wg = wm[:, g * GQA_HEAD_DIM:(g + 1) * GQA_HEAD_DIM]
            out += [wg, z64, z64, wg]
        return jnp.concatenate(out, 1)

    kr_slab = lambda wm: jnp.zeros((D_MODEL, LANE), F32).at[:, MLA_NOPE:MLA_NOPE + MLA_ROPE].set(wm)
    w = jnp.concatenate([w_cq, w_ckv, w_gq, w_gq[:, head_perm(GQA_HEADS)],
                         placed(w_gk), placed(w_gk[:, head_perm(GQA_KV_HEADS)]),
                         kr_slab(w_kr), kr_slab(w_kr[:, p32]),
                         w_gv[:, :GQA_HEAD_DIM], z64, w_gv[:, GQA_HEAD_DIM:], z64], 1).astype(BF16)
    splits = [(MLA_Q_LORA, F32), (MLA_KV_LORA, F32), (512, F32), (512, F32), (512, F32), (512, F32),
              (LANE, F32), (LANE, F32), (2 * LANE, BF16)]
    uq = w_uq.reshape(MLA_Q_LORA, MLA_HEADS, MLA_NOPE + MLA_ROPE)
    pad_q = jnp.zeros((MLA_Q_LORA, MLA_HEADS, LANE - MLA_NOPE - MLA_ROPE), F32)
    wq = jnp.concatenate([uq, pad_q], 2).reshape(MLA_Q_LORA, D_MODEL)
    uq_par = jnp.concatenate([jnp.zeros_like(uq[:, :, :MLA_NOPE]), uq[:, :, MLA_NOPE:][:, :, p32]], 2)
    wqp = jnp.concatenate([uq_par, pad_q], 2).reshape(MLA_Q_LORA, D_MODEL)
    ukv = w_ukv.reshape(MLA_KV_LORA, MLA_HEADS, MLA_NOPE + MLA_V)
    zk = jnp.zeros((MLA_KV_LORA, MLA_HEADS, MLA_NOPE), F32)
    wk = jnp.concatenate([ukv[:, :, :MLA_NOPE], zk], 2).reshape(MLA_KV_LORA, D_MODEL)
    uv = ukv[:, :, MLA_NOPE:]
    wv = jnp.concatenate([uv, jnp.zeros_like(uv)], 2).reshape(MLA_KV_LORA, D_MODEL)
    bd = np.kron(np.eye(8, dtype=np.float32), np.full((64, 64), 1.0 / 64, np.float32))
    tile = lambda v, n: jnp.tile(v, n).reshape(1, -1)
    return dict(
        w=w, splits=splits,
        norms=[q_norm.reshape(1, -1), kv_norm.reshape(1, -1), tile(gq_norm, 8), tile(gq_norm[p64], 8),
               tile(gk_norm, 8), tile(gk_norm[p64], 8)],
        weights=[wq.astype(BF16), wqp.astype(BF16), wk.astype(BF16), wv.astype(BF16), jnp.asarray(bd, BF16)],
        w_out=[w_out[:512].astype(BF16), w_out[512:].astype(BF16)])


def kernel(x, c, ctx, c_ctx, ada_w, ada_b, ln_g, ln_b, e_w_in, e_w_out, e_conv_w, e_conv_b, e_dt_bias, e_a_log, e_d_skip, e_gnorm_w, e_rpb, o_w_in, o_w_out, o_mla_q_norm, o_mla_kv_norm, o_w_uq, o_w_ukv, o_gqa_q_norm, o_gqa_k_norm, moe_wg, moe_bg, moe_we, moe_be, moe_w1, moe_w3, moe_w2):
    b, l, dm = x.shape
    cl = ctx.shape[1]
    depth = ada_w.shape[0]
    lay = _Layout(b, cl, l)
    alpha = (2.0 * depth) ** 0.25
    xa = jnp.concatenate([ctx.reshape(b * cl, dm), x.reshape(b * l, dm)], 0)
    cvec = jnp.zeros((MOD_ROWS, dm), F32).at[:b].set(c).at[b].set(c_ctx)
    mods = ada_mod(cvec, ada_w, ada_b).reshape(depth, MOD_ROWS, N_MOD, 1, dm).transpose(0, 2, 1, 3, 4)
    rope_tabs = _rope_tables(lay)
    row = lambda v: v.reshape(1, -1)
    for layer in range(depth):
        i = layer // 2
        mod = [mods[layer, k] for k in range(N_MOD)]
        if layer % 2 == 0:
            ew = _even_weights(e_w_in[i], e_w_out[i], e_conv_w[i], e_conv_b[i], e_dt_bias[i], e_a_log[i],
                               e_d_skip[i], e_gnorm_w[i])
            qkv, z, xbc, dt, dtt = fused_proj(
                lay, xa, mod[0], mod[1], ew['w'],
                [(3 * NA_DIM, BF16), (SSD_INNER, F32), (XBC_DIM, F32), (2 * LANE, F32)], wt=ew['wt'])
            na = na_attention(lay, qkv, na_bias_table(e_rpb[i]))
            u = conv_silu(lay, xbc, ew['conv_w'], ew['conv_b'])
            yf = ssd_pass(lay, False, u, dt, dtt, ew['bias_nat'][0], ew['a_nat'][0], ew['bias_col'][0],
                          ew['a_col'][0], ew['expand'])
            ssd = ssd_pass(lay, True, u, dt, dtt, ew['bias_nat'][1], ew['a_nat'][1], ew['bias_col'][1],
                           ew['a_col'][1], ew['expand'], extra=(yf, z, ew['dskip'], ew['gnorm']))
            mixes, w_outs = [na, ssd], ew['w_out']
        else:
            ow = _odd_weights(o_w_in[i], o_w_out[i], o_mla_q_norm[i], o_mla_kv_norm[i], o_w_uq[i], o_w_ukv[i],
                              o_gqa_q_norm[i], o_gqa_k_norm[i])
            cq, ckv, gq, gqp, gk, gkp, kra, krb, gv = fused_proj(lay, xa, mod[0], mod[1], ow['w'], ow['splits'])
            mq, mk, mv, gqr, gkr, gvo = odd_prep(lay, [cq, ckv, gq, gqp, gk, gkp, kra, krb, gv], ow['norms'],
                                                 ow['weights'], rope_tabs)
            o_mla = dense_attention(lay, mq, mk, mv, False, lambda p: p)
            o_gqa = dense_attention(lay, gqr, gkr, gvo, True, lambda p: p // 2)
            mixes, w_outs = [o_mla, o_gqa], ow['w_out']
        wr = jnp.zeros((dm, LANE), F32).at[:, :MOE_GROUPS].set(moe_wg[layer])
        wr = wr.at[:, MOE_GROUPS:MOE_GROUPS + MOE_EXPERTS].set(moe_we[layer])
        wr_hi = wr.astype(BF16)
        wr_lo = (wr - wr_hi.astype(F32)).astype(BF16)
        br = jnp.zeros((1, LANE), F32).at[0, :MOE_GROUPS].set(moe_bg[layer])
        br = br.at[0, MOE_GROUPS:MOE_GROUPS + MOE_EXPERTS].set(moe_be[layer])
        xa, h2, logits = outproj_ln(lay, xa, mixes, w_outs, mod[2], row(ln_g[layer, 0]), row(ln_b[layer, 0]),
                                    mod[3], mod[4], wr_hi, wr_lo, br, alpha)
        expert, gate = moe_route(logits)
        blk_e, n_used, slot_tok, slot_w, pos = moe_dispatch_plan(expert, gate)
        y = moe_ffn(h2, blk_e, n_used, slot_tok, slot_w, moe_w1, moe_w3, moe_w2, layer)
        xa = moe_combine_ln(lay, pos, y, xa, mod[5], row(ln_g[layer, 1]), row(ln_b[layer, 1]), alpha)
    return xa[lay.nc:].reshape(b, l, dm)
```
